```python
import jax
import jax.numpy as jnp
from jax import lax
import numpy as np

D_MODEL = 2048
BATCH = 4
SEQ = 4096
DEPTH = 2

HEAD_DIM = 64
N_MIX_HEADS = D_MODEL // HEAD_DIM
A_HEADS = 3 * N_MIX_HEADS // 8
B_GROUPS = N_MIX_HEADS // 4
C_HEADS = N_MIX_HEADS - A_HEADS - B_GROUPS
A_WIDTH = A_HEADS * HEAD_DIM
B_WIDTH = B_GROUPS * HEAD_DIM
C_WIDTH = C_HEADS * HEAD_DIM
D_MIX = A_WIDTH + B_WIDTH + C_WIDTH

DILATED_PATTERNS = ((128, 1), (512, 4), (2048, 16))
BLOCK = 128
REL_BUCKETS = 32
REL_MAX_DISTANCE = 2048

CHUNK = 128

W_LORA = 64
A_LORA = 64
G_LORA = 256
C_LORA = W_LORA + A_LORA + G_LORA
C_IN_COLS = 3 * C_WIDTH + C_LORA
RWKV_GN_EPS = 64e-5

MEM_TOKENS = 256
MEM_HEADS = 4
MEM_HEAD_DIM = 128
MEM_INNER = MEM_HEADS * MEM_HEAD_DIM

D_FF = 5632
CONV_WIDTH = 3

IN_COLS = 3 * A_WIDTH + 2 * B_WIDTH + C_IN_COLS
NORM_EPS = 1e-6
NEG_INF = -1e30

kernel_name = 'hybrid_dilated_sgu_rwkv7_block'


def rmsnorm(x, gain):
    xf = x.astype(jnp.float32)
    y = xf * lax.rsqrt(jnp.mean(xf * xf, axis=-1, keepdims=True) + NORM_EPS)
    return (y * gain.astype(jnp.float32)).astype(x.dtype)


def layernorm(x, gain):
    xf = x.astype(jnp.float32)
    xc = xf - jnp.mean(xf, axis=-1, keepdims=True)
    y = xc * lax.rsqrt(jnp.mean(xc * xc, axis=-1, keepdims=True) + NORM_EPS)
    return (y * gain.astype(jnp.float32)).astype(x.dtype)


def t5_causal_bucket(dist):
    max_exact = REL_BUCKETS // 2
    d = np.maximum(dist, 0)
    scaled = np.log(np.maximum(d, 1) / max_exact) / np.log(REL_MAX_DISTANCE / max_exact)
    large = np.minimum(max_exact + (scaled * (REL_BUCKETS - max_exact)).astype(np.int32), REL_BUCKETS - 1)
    return np.where(d < max_exact, d, large).astype(np.int32)


def dilated_pattern(q, k, v, rel_bias_table, window, dilation):
    b, s, h, dh = q.shape
    steps = window // dilation
    sub_len = s // dilation
    n_blk = -(-sub_len // BLOCK)
    pad = n_blk * BLOCK - sub_len

    def strided(z):
        z = z.reshape(b, sub_len, dilation, h, dh).transpose(0, 2, 3, 1, 4)
        z = jnp.pad(z, ((0, 0), (0, 0), (0, 0), (0, pad), (0, 0)))
        return z.reshape(b, dilation, h, n_blk, BLOCK, dh)

    def with_prev(z):
        prev = jnp.pad(z, ((0, 0), (0, 0), (0, 0), (1, 0), (0, 0), (0, 0)))[:, :, :, :-1]
        return jnp.concatenate([prev, z], axis=4)

    qb = strided(q)
    kw = with_prev(strided(k))
    vw = with_prev(strided(v))

    qi = np.arange(BLOCK)[:, None]
    kj = np.arange(2 * BLOCK)[None, :]
    delta = qi + BLOCK - kj
    band = (delta >= 0) & (delta <= steps)
    valid = np.where((np.arange(n_blk) == 0)[:, None, None], band & (kj >= BLOCK), band)
    bucket = t5_causal_bucket(np.clip(delta, 0, steps) * dilation)
    bias = jnp.transpose(rel_bias_table.astype(jnp.float32)[bucket], (2, 0, 1))

    scores = jnp.einsum('brhnqd,brhnkd->brhnqk', qb, kw).astype(jnp.float32) * (dh ** -0.5)
    scores = jnp.where(valid, scores + bias[None, None, :, None], NEG_INF)
    m = jnp.max(scores, axis=-1, keepdims=True)
    p = jnp.exp(scores - m)
    den = jnp.sum(p, axis=-1, keepdims=True)
    o = jnp.einsum('brhnqk,brhnkd->brhnqd', p, vw.astype(jnp.float32)) / den
    lse = (m + jnp.log(den))[..., 0]

    def unstrided(z):
        z = z.reshape(b, dilation, h, n_blk * BLOCK, *z.shape[5:])[:, :, :, :sub_len]
        z = jnp.moveaxis(z, 3, 1)
        return z.reshape(b, s, h, *z.shape[4:])

    return unstrided(o), unstrided(lse)


def dilated_attention(q, k, v, rel_bias_table):
    outs, lses = [], []
    for window, dilation in DILATED_PATTERNS:
        o, lse = dilated_pattern(q, k, v, rel_bias_table, window, dilation)
        outs.append(o)
        lses.append(lse)
    weights = jax.nn.softmax(jnp.stack(lses), axis=0)
    return jnp.sum(weights[..., None] * jnp.stack(outs), axis=0)


def spatial_gating(zb, norm_gain, w_s, b_s):
    b, s, _ = zb.shape
    u, g = jnp.split(jax.nn.gelu(zb), 2, axis=-1)
    g = layernorm(g, norm_gain).reshape(b, s // CHUNK, CHUNK, B_GROUPS, HEAD_DIM)
    w_causal = jnp.where(np.tril(np.ones((CHUNK, CHUNK), dtype=bool)), w_s, 0)
    mixed = jnp.einsum('gij,bcjgd->bcigd', w_causal, g) + b_s.T[None, None, :, :, None]
    return u * mixed.reshape(b, s, B_WIDTH)


def rwkv7_scan(r, decay, k, v, kk, a):
    b, s, h, n = r.shape

    def step(state, inp):
        r_t, w_t, k_t, v_t, kk_t, a_t = inp
        sa = jnp.einsum('bhvk,bhk->bhv', state, -kk_t)
        state = (state * w_t[:, :, None, :] + sa[..., None] * (kk_t * a_t)[:, :, None, :]
                 + v_t[..., None] * k_t[:, :, None, :])
        return state, jnp.einsum('bhvk,bhk->bhv', state, r_t)

    xs = tuple(jnp.moveaxis(z, 1, 0) for z in (r, decay, k, v, kk, a))
    _, ys = lax.scan(step, jnp.zeros((b, h, n, n), jnp.float32), xs)
    return jnp.moveaxis(ys, 0, 1)


def rwkv7_time_mix(zc, mu, w0, w_up, a0, a_up, g_up, k_k, k_a, r_k, ln_gain, ln_bias):
    b, s, _ = zc.shape
    zc = zc.astype(jnp.float32)
    prev = jnp.pad(zc, ((0, 0), (1, 0), (0, 0)))[:, :-1]
    zc = zc + (prev - zc) * mu
    cuts = [C_WIDTH, 2 * C_WIDTH, 3 * C_WIDTH, 3 * C_WIDTH + W_LORA, 3 * C_WIDTH + W_LORA + A_LORA]
    r, k, v, w_lo, a_lo, g_lo = jnp.split(zc, cuts, axis=-1)
    w_log = -jax.nn.softplus(-(w0 + jnp.tanh(w_lo) @ w_up)) - 0.5
    decay = jnp.exp(-jnp.exp(w_log))
    a = jax.nn.sigmoid(a0 + a_lo @ a_up)
    g = jax.nn.sigmoid(g_lo) @ g_up
    kk = k * k_k
    k = k * (1.0 + (a - 1.0) * k_a)

    def heads(z):
        return z.reshape(b, s, C_HEADS, HEAD_DIM)

    r, k, v, kk, a, decay = heads(r), heads(k), heads(v), heads(kk), heads(a), heads(decay)
    kk = kk / jnp.maximum(jnp.linalg.norm(kk, axis=-1, keepdims=True), 1e-12)
    y = rwkv7_scan(r, decay, k, v, kk, a)
    mean = jnp.mean(y, axis=-1, keepdims=True)
    yc = y - mean
    y = yc * lax.rsqrt(jnp.mean(yc * yc, axis=-1, keepdims=True) + RWKV_GN_EPS)
    y = y * ln_gain.reshape(C_HEADS, HEAD_DIM) + ln_bias.reshape(C_HEADS, HEAD_DIM)
    y = y + jnp.sum(r * k * r_k, axis=-1, keepdims=True) * v
    return y.reshape(b, s, C_WIDTH) * g


def hybrid_mixer(h, rel_bias_table, w_in, w_out, attn_out_gain, sgu_norm_gain, sgu_w, sgu_b,
                 sgu_out_gain, rwkv_mu, rwkv_w0, rwkv_w_up, rwkv_a0, rwkv_a_up, rwkv_g_up,
                 rwkv_k_k, rwkv_k_a, rwkv_r_k, rwkv_ln_gain, rwkv_ln_bias):
    b, s, _ = h.shape
    z = h @ w_in
    za, zb, zc = jnp.split(z, [3 * A_WIDTH, 3 * A_WIDTH + 2 * B_WIDTH], axis=-1)
    qkv = za.reshape(b, s, 3, A_HEADS, HEAD_DIM)
    oa = dilated_attention(qkv[:, :, 0], qkv[:, :, 1], qkv[:, :, 2], rel_bias_table)
    oa = rmsnorm(oa.reshape(b, s, A_WIDTH), attn_out_gain).astype(h.dtype)
    ob = rmsnorm(spatial_gating(zb, sgu_norm_gain, sgu_w, sgu_b), sgu_out_gain).astype(h.dtype)
    oc = rwkv7_time_mix(zc, rwkv_mu, rwkv_w0, rwkv_w_up, rwkv_a0, rwkv_a_up, rwkv_g_up,
                        rwkv_k_k, rwkv_k_a, rwkv_r_k, rwkv_ln_gain, rwkv_ln_bias).astype(h.dtype)
    return jnp.concatenate([oa, ob, oc], axis=-1) @ w_out


def memory_cross_attention(h, mem_n, wq, wkv, wo):
    b, s, _ = h.shape
    q = (h @ wq).reshape(b, s, MEM_HEADS, MEM_HEAD_DIM)
    kv = (mem_n @ wkv).reshape(b, mem_n.shape[1], 2, MEM_HEADS, MEM_HEAD_DIM)
    scores = jnp.einsum('bshd,bmhd->bhsm', q, kv[:, :, 0]).astype(jnp.float32) * (MEM_HEAD_DIM ** -0.5)
    p = jax.nn.softmax(scores, axis=-1).astype(h.dtype)
    o = jnp.einsum('bhsm,bmhd->bshd', p, kv[:, :, 1]).reshape(b, s, MEM_INNER)
    return o @ wo


def conv_ffn(h, w_up, conv_w, conv_b, w_down):
    s = h.shape[1]
    up = h @ w_up
    up_pad = jnp.pad(up, ((0, 0), (CONV_WIDTH - 1, 0), (0, 0)))
    conv = conv_b + sum(up_pad[:, j:j + s] * conv_w[j] for j in range(CONV_WIDTH))
    gate, val = jnp.split(conv, 2, axis=-1)
    return (jax.nn.gelu(gate, approximate=True) * val) @ w_down


def setup_inputs(seed: int = 0) -> dict:
    key = jax.random.key(seed)
    keys = jax.random.split(key, 40)
    counter = [0]

    def nxt():
        kk = keys[counter[0]]
        counter[0] += 1
        return kk

    def nrm(shape, scale=1.0):
        return scale * jax.random.normal(nxt(), shape, jnp.float32)

    def gain(shape):
        return 1.0 + nrm(shape, 0.1)

    L = DEPTH
    return {
        'x': nrm((BATCH, SEQ, D_MODEL)),
        'mem': nrm((BATCH, MEM_TOKENS, D_MODEL)),
        'rel_bias_table': nrm((REL_BUCKETS, A_HEADS), 0.5),
        'sandwich_gains': gain((L, 6, D_MODEL)),
        'mem_src_gain': gain((L, D_MODEL)),
        'w_in': nrm((L, D_MODEL, IN_COLS), D_MODEL ** -0.5),
        'w_out': nrm((L, D_MIX, D_MODEL), D_MIX ** -0.5),
        'attn_out_gain': gain((L, A_WIDTH)),
        'sgu_norm_gain': gain((L, B_WIDTH)),
        'sgu_w': nrm((L, B_GROUPS, CHUNK, CHUNK), CHUNK ** -0.5),
        'sgu_b': gain((L, B_GROUPS, CHUNK)),
        'sgu_out_gain': gain((L, B_WIDTH)),
        'rwkv_mu': jax.random.uniform(nxt(), (L, C_IN_COLS), jnp.float32),
        'rwkv_w0': nrm((L, C_WIDTH), 0.5),
        'rwkv_w_up': nrm((L, W_LORA, C_WIDTH), W_LORA ** -0.5),
        'rwkv_a0': nrm((L, C_WIDTH), 0.1),
        'rwkv_a_up': nrm((L, A_LORA, C_WIDTH), A_LORA ** -0.5),
        'rwkv_g_up': nrm((L, G_LORA, C_WIDTH), G_LORA ** -0.5),
        'rwkv_k_k': gain((L, C_WIDTH)),
        'rwkv_k_a': gain((L, C_WIDTH)),
        'rwkv_r_k': nrm((L, C_HEADS, HEAD_DIM), 0.1),
        'rwkv_ln_gain': gain((L, C_WIDTH)),
        'rwkv_ln_bias': nrm((L, C_WIDTH), 0.01),
        'mem_wq': nrm((L, D_MODEL, MEM_INNER), D_MODEL ** -0.5),
        'mem_wkv': nrm((L, D_MODEL, 2 * MEM_INNER), D_MODEL ** -0.5),
        'mem_wo': nrm((L, MEM_INNER, D_MODEL), MEM_INNER ** -0.5),
        'ffn_w_up': nrm((L, D_MODEL, 2 * D_FF), D_MODEL ** -0.5),
        'ffn_conv_w': nrm((L, CONV_WIDTH, 2 * D_FF), CONV_WIDTH ** -0.5),
        'ffn_conv_b': nrm((L, 2 * D_FF), 0.01),
        'ffn_w_down': nrm((L, D_FF, D_MODEL), D_FF ** -0.5),
    }


def reference(x, mem, rel_bias_table, sandwich_gains, mem_src_gain, w_in, w_out, attn_out_gain,
              sgu_norm_gain, sgu_w, sgu_b, sgu_out_gain, rwkv_mu, rwkv_w0, rwkv_w_up, rwkv_a0,
              rwkv_a_up, rwkv_g_up, rwkv_k_k, rwkv_k_a, rwkv_r_k, rwkv_ln_gain, rwkv_ln_bias,
              mem_wq, mem_wkv, mem_wo, ffn_w_up, ffn_conv_w, ffn_conv_b, ffn_w_down):
    for l in range(DEPTH):
        g = sandwich_gains[l]
        h = rmsnorm(x, g[0])
        y = hybrid_mixer(h, rel_bias_table, w_in[l], w_out[l], attn_out_gain[l], sgu_norm_gain[l],
                         sgu_w[l], sgu_b[l], sgu_out_gain[l], rwkv_mu[l], rwkv_w0[l], rwkv_w_up[l],
                         rwkv_a0[l], rwkv_a_up[l], rwkv_g_up[l], rwkv_k_k[l], rwkv_k_a[l], rwkv_r_k[l],
                         rwkv_ln_gain[l], rwkv_ln_bias[l])
        x = x + rmsnorm(y, g[1])
        h = rmsnorm(x, g[2])
        y = memory_cross_attention(h, rmsnorm(mem, mem_src_gain[l]), mem_wq[l], mem_wkv[l], mem_wo[l])
        x = x + rmsnorm(y, g[3])
        h = rmsnorm(x, g[4])
        y = conv_ffn(h, ffn_w_up[l], ffn_conv_w[l], ffn_conv_b[l], ffn_w_down[l])
        x = x + rmsnorm(y, g[5])
    return x
```

```python
import functools

import numpy as np
import jax
import jax.numpy as jnp
from jax import lax
from jax.experimental import pallas as pl
from jax.experimental.pallas import tpu as pltpu

HEAD_DIM = 64
LANES = 128
DILATED_PATTERNS = ((128, 1), (512, 4), (2048, 16))
BLOCK = 128
REL_BUCKETS = 32
REL_MAX_DISTANCE = 2048
CHUNK = 128
RWKV_CHUNK = 64
W_LORA = 64
A_LORA = 64
G_LORA = 256
RWKV_GN_EPS = 64e-5
MEM_HEAD_DIM = 128
CONV_WIDTH = 3
NORM_EPS = 1e-6
NEG_INF = -1e30
HALO = 8

F32 = jnp.float32
BF16 = jnp.bfloat16
HIGHEST = lax.Precision.HIGHEST
VMEM_LIMIT_BYTES = 56 * 1024 * 1024


def _params(*sem):
    return pltpu.CompilerParams(dimension_semantics=sem, vmem_limit_bytes=VMEM_LIMIT_BYTES)


def _dot(a, b):
    return jnp.dot(a, b, preferred_element_type=F32)


def _dot_nt(a, b):
    return lax.dot_general(a, b, (((1,), (1,)), ((), ())), preferred_element_type=F32)


def _dot_tn(a, b):
    return lax.dot_general(a, b, (((0,), (0,)), ((), ())), preferred_element_type=F32)


def _dot_f32(a, b):
    return jnp.dot(a, b, preferred_element_type=F32, precision=HIGHEST)


def _rms_rows(x, gain):
    ms = jnp.mean(x * x, axis=-1, keepdims=True)
    return x * lax.rsqrt(ms + NORM_EPS) * gain


def _gelu_tanh(x):
    return 0.5 * x * (1.0 + jnp.tanh(np.sqrt(2.0 / np.pi).astype(np.float32) * (x + 0.044715 * (x * x * x))))


def _sigmoid(x):
    return 1.0 / (1.0 + jnp.exp(-x))


def _head_sum_matrix():
    r = lax.broadcasted_iota(jnp.int32, (LANES, LANES), 0) // HEAD_DIM
    c = lax.broadcasted_iota(jnp.int32, (LANES, LANES), 1) // HEAD_DIM
    return (r == c).astype(F32)


def _rms_matmul_kernel(x_ref, g_ref, w_ref, o_ref, h_ref, *, sub):
    @pl.when(pl.program_id(1) == 0)
    def _():
        def body(i, c):
            rows = pl.ds(pl.multiple_of(i * sub, sub), sub)
            h_ref[rows, :] = _rms_rows(x_ref[rows, :], g_ref[...]).astype(BF16)
            return c
        lax.fori_loop(0, x_ref.shape[0] // sub, body, 0)

    o_ref[...] = _dot(h_ref[...], w_ref[...]).astype(o_ref.dtype)


def rms_matmul(x, gain, w, tm, tn, out_dtype=F32):
    m, k = x.shape
    n = w.shape[1]
    assert m % tm == 0 and n % tn == 0
    sub = min(tm, 256)
    return pl.pallas_call(
        functools.partial(_rms_matmul_kernel, sub=sub),
        grid=(m // tm, n // tn),
        in_specs=[pl.BlockSpec((tm, k), lambda i, j: (i, 0)),
                  pl.BlockSpec((1, k), lambda i, j: (0, 0)),
                  pl.BlockSpec((k, tn), lambda i, j: (0, j))],
        out_specs=pl.BlockSpec((tm, tn), lambda i, j: (i, j)),
        out_shape=jax.ShapeDtypeStruct((m, n), out_dtype),
        scratch_shapes=[pltpu.VMEM((tm, k), BF16)],
        compiler_params=_params("parallel", "arbitrary"),
        name="rms_matmul",
    )(x, gain.reshape(1, k), w)


def _t5_causal_bucket(dist):
    max_exact = REL_BUCKETS // 2
    d = np.maximum(dist, 0)
    scaled = np.log(np.maximum(d, 1) / max_exact) / np.log(REL_MAX_DISTANCE / max_exact)
    large = np.minimum(max_exact + (scaled * (REL_BUCKETS - max_exact)).astype(np.int32), REL_BUCKETS - 1)
    return np.where(d < max_exact, d, large).astype(np.int32)


def _bucket_tables():
    out = []
    qi = np.arange(BLOCK)[:, None]
    kj = np.arange(2 * BLOCK)[None, :]
    for window, dilation in DILATED_PATTERNS:
        steps = window // dilation
        assert steps <= BLOCK
        delta = qi + BLOCK - kj
        band = (delta >= 0) & (delta <= steps)
        bucket = _t5_causal_bucket(np.clip(delta, 0, steps) * dilation)
        later = np.where(band, bucket, -1)
        first = np.where(band & (kj >= BLOCK), bucket, -1)
        out.append(np.stack([later, first]))
    return np.stack(out).astype(np.int32)


def _dilated_attn_kernel(table_ref, bucket_ref, q_ref, k_ref, v_ref, o_ref, bias_ref, acc_ref, lse_ref,
                         *, seq, n_pairs):
    hp = pl.program_id(0) % n_pairs
    lane = lax.broadcasted_iota(jnp.int32, (BLOCK, LANES), 1)
    lo_half = lane < HEAD_DIM

    for p in range(len(DILATED_PATTERNS)):
        for var in range(2):
            bucket = bucket_ref[p, var]
            for h in range(2):
                val = jnp.zeros(bucket.shape, F32)
                for b in range(REL_BUCKETS):
                    val = jnp.where(bucket == b, table_ref[b, 2 * hp + h], val)
                bias_ref[p, h, var] = jnp.where(bucket < 0, NEG_INF, val)

    scale = HEAD_DIM ** -0.5
    for p, (window, d) in enumerate(DILATED_PATTERNS):
        n_blk = seq // (d * BLOCK)
        shift = int(np.log2(d))

        def block(idx, carry, p=p, d=d, shift=shift):
            r = idx & (d - 1)
            n = idx >> shift
            base = r + n * (BLOCK * d)
            prev = r + jnp.maximum(n - 1, 0) * (BLOCK * d)
            if d == 1:
                cur_rows = pl.ds(pl.multiple_of(base, BLOCK), BLOCK)
                prev_rows = pl.ds(pl.multiple_of(prev, BLOCK), BLOCK)
            else:
                cur_rows = pl.ds(base, BLOCK, stride=d)
                prev_rows = pl.ds(prev, BLOCK, stride=d)
            q = q_ref[cur_rows, :] * scale
            kcat = jnp.concatenate([k_ref[prev_rows, :], k_ref[cur_rows, :]], axis=0).astype(BF16)
            vcat = jnp.concatenate([v_ref[prev_rows, :], v_ref[cur_rows, :]], axis=0).astype(BF16)
            var = jnp.where(n == 0, 1, 0)
            outs, lses = [], []
            for h in range(2):
                qh = jnp.where(lo_half if h == 0 else ~lo_half, q, 0.0).astype(BF16)
                s = _dot_nt(qh, kcat) + bias_ref[p, h, var]
                m = jnp.max(s, axis=-1, keepdims=True)
                e = jnp.exp(s - m)
                den = jnp.sum(e, axis=-1, keepdims=True)
                outs.append(_dot(e.astype(BF16), vcat) / den)
                lses.append(m + jnp.log(den))
            acc_ref[p, cur_rows, :] = jnp.where(lo_half, outs[0], outs[1])
            lse_ref[p, cur_rows, :] = jnp.where(lo_half, lses[0], lses[1])
            return carry

        lax.fori_loop(0, d * n_blk, block, 0)

    def merge(i, carry):
        rows = pl.ds(pl.multiple_of(i * BLOCK, BLOCK), BLOCK)
        l0, l1, l2 = lse_ref[0, rows, :], lse_ref[1, rows, :], lse_ref[2, rows, :]
        mx = jnp.maximum(jnp.maximum(l0, l1), l2)
        w0, w1, w2 = jnp.exp(l0 - mx), jnp.exp(l1 - mx), jnp.exp(l2 - mx)
        num = w0 * acc_ref[0, rows, :] + w1 * acc_ref[1, rows, :] + w2 * acc_ref[2, rows, :]
        o_ref[rows, :] = num / (w0 + w1 + w2)
        return carry

    lax.fori_loop(0, seq // BLOCK, merge, 0)


def dilated_attention(za, rel_bias_table, a_heads):
    b, s, _ = za.shape
    n_pairs = a_heads // 2
    assert s % (DILATED_PATTERNS[-1][1] * BLOCK) == 0
    buckets = jnp.asarray(_bucket_tables())
    n_pat = len(DILATED_PATTERNS)
    return pl.pallas_call(
        functools.partial(_dilated_attn_kernel, seq=s, n_pairs=n_pairs),
        grid=(b * n_pairs,),
        in_specs=[pl.BlockSpec(memory_space=pltpu.SMEM),
                  pl.BlockSpec((n_pat, 2, BLOCK, 2 * BLOCK), lambda g: (0, 0, 0, 0)),
                  pl.BlockSpec((None, s, LANES), lambda g: (g // n_pairs, 0, g % n_pairs)),
                  pl.BlockSpec((None, s, LANES), lambda g: (g // n_pairs, 0, n_pairs + g % n_pairs)),
                  pl.BlockSpec((None, s, LANES), lambda g: (g // n_pairs, 0, 2 * n_pairs + g % n_pairs))],
        out_specs=pl.BlockSpec((None, s, LANES), lambda g: (g // n_pairs, 0, g % n_pairs)),
        out_shape=jax.ShapeDtypeStruct((b, s, a_heads * HEAD_DIM), F32),
        scratch_shapes=[pltpu.VMEM((n_pat, 2, 2, BLOCK, 2 * BLOCK), F32),
                        pltpu.VMEM((n_pat, s, LANES), F32),
                        pltpu.VMEM((n_pat, s, LANES), F32)],
        compiler_params=_params("parallel"),
        name="dilated_attention",
    )(rel_bias_table, buckets, za, za, za)


def _sgu_kernel(z_ref, ng_ref, w_ref, b_ref, og_ref, o_ref, *, width, groups):
    z = _gelu_tanh(z_ref[...])
    u = z[:, :width]
    g = z[:, width:]
    gc = g - jnp.mean(g, axis=-1, keepdims=True)
    gn = gc * lax.rsqrt(jnp.mean(gc * gc, axis=-1, keepdims=True) + NORM_EPS) * ng_ref[...]
    ri = lax.broadcasted_iota(jnp.int32, (CHUNK, CHUNK), 0)
    ci = lax.broadcasted_iota(jnp.int32, (CHUNK, CHUNK), 1)
    causal = ri >= ci
    lane = lax.broadcasted_iota(jnp.int32, (CHUNK, LANES), 1)
    lo_half = lane < HEAD_DIM
    rows = z.shape[0]
    cols = []
    for j in range(groups // 2):
        w0 = jnp.where(causal, w_ref[2 * j], 0.0).astype(BF16)
        w1 = jnp.where(causal, w_ref[2 * j + 1], 0.0).astype(BF16)
        parts = []
        for c in range(rows // CHUNK):
            blk = gn[c * CHUNK:(c + 1) * CHUNK, j * LANES:(j + 1) * LANES]
            g0 = jnp.where(lo_half, blk, 0.0).astype(BF16)
            g1 = jnp.where(lo_half, 0.0, blk).astype(BF16)
            parts.append(_dot(w0, g0) + _dot(w1, g1) + b_ref[:, j * LANES:(j + 1) * LANES])
        cols.append(jnp.concatenate(parts, axis=0))
    mixed = jnp.concatenate(cols, axis=1)
    o_ref[...] = _rms_rows(u * mixed, og_ref[...]).astype(o_ref.dtype)


def spatial_gating(zb, norm_gain, w_s, b_s, out_gain, tm):
    m, two_w = zb.shape
    width = two_w // 2
    groups = w_s.shape[0]
    bias = jnp.repeat(b_s.T, HEAD_DIM, axis=1)
    return pl.pallas_call(
        functools.partial(_sgu_kernel, width=width, groups=groups),
        grid=(m // tm,),
        in_specs=[pl.BlockSpec((tm, two_w), lambda i: (i, 0)),
                  pl.BlockSpec((1, width), lambda i: (0, 0)),
                  pl.BlockSpec((groups, CHUNK, CHUNK), lambda i: (0, 0, 0)),
                  pl.BlockSpec((CHUNK, width), lambda i: (0, 0)),
                  pl.BlockSpec((1, width), lambda i: (0, 0))],
        out_specs=pl.BlockSpec((tm, width), lambda i: (i, 0)),
        out_shape=jax.ShapeDtypeStruct((m, width), BF16),
        compiler_params=_params("parallel"),
        name="spatial_gating",
    )(zb, norm_gain.reshape(1, width), w_s, bias, out_gain.reshape(1, width))


def _stack_heads(z, lo_half):
    return jnp.concatenate([jnp.where(lo_half, z, 0.0), jnp.where(lo_half, 0.0, z)], axis=0)


def _rwkv_chunk_kernel(r_ref, k_ref, v_ref, lo_ref, g0_ref, g1_ref,
                       rp_ref, kp_ref, vp_ref, lop_ref, g0p_ref, g1p_ref,
                       mu_ref, mulo_ref, mug_ref, w0_ref, wup_ref, a0_ref, aup_ref, gup_ref,
                       kk_ref, ka_ref, rk_ref,
                       mc_ref, qp_ref, nc_ref, y1_ref, bonus_ref, gate_ref,
                       rs_ref, ks_ref, vs_ref, los_ref, gs_ref, *, ts):
    c = RWKV_CHUNK
    first = pl.program_id(2) == 0
    row = lax.broadcasted_iota(jnp.int32, (ts, LANES), 0)

    def shifted(x_ref, halo_ref, mu):
        x = x_ref[...]
        before = jnp.where(first, 0.0, halo_ref[HALO - 1:HALO, :])
        prev = jnp.where(row == 0, before, pltpu.roll(x, 1, axis=0))
        return x + (prev - x) * mu

    rs_ref[...] = shifted(r_ref, rp_ref, mu_ref[0:1, :])
    ks_ref[...] = shifted(k_ref, kp_ref, mu_ref[1:2, :])
    vs_ref[...] = shifted(v_ref, vp_ref, mu_ref[2:3, :])
    los_ref[...] = shifted(lo_ref, lop_ref, mulo_ref[...])
    gs_ref[:, :LANES] = shifted(g0_ref, g0p_ref, mug_ref[:, :LANES])
    gs_ref[:, LANES:] = shifted(g1_ref, g1p_ref, mug_ref[:, LANES:])

    head_sum = _head_sum_matrix()
    lane = lax.broadcasted_iota(jnp.int32, (c, LANES), 1)
    lo_half = lane < HEAD_DIM
    ri = lax.broadcasted_iota(jnp.int32, (2 * c, 2 * c), 0)
    ci = lax.broadcasted_iota(jnp.int32, (2 * c, 2 * c), 1)
    same = (ri // c) == (ci // c)
    strict = same & ((ri % c) > (ci % c))
    incl = same & ((ri % c) >= (ci % c))
    diag = ri == ci
    ti = lax.broadcasted_iota(jnp.int32, (c, c), 0)
    tj = lax.broadcasted_iota(jnp.int32, (c, c), 1)
    tril = (ti >= tj).astype(F32)

    def chunk(j, carry):
        rows = pl.ds(pl.multiple_of(j * c, c), c)
        r = rs_ref[rows, :]
        kx = ks_ref[rows, :]
        v = vs_ref[rows, :]
        lo = los_ref[rows, :]
        gl = gs_ref[rows, :]

        z = w0_ref[...] + _dot(jnp.tanh(lo).astype(BF16), wup_ref[...])
        w_log = -(jnp.maximum(-z, 0.0) + jnp.log(1.0 + jnp.exp(-jnp.abs(z)))) - 0.5
        e = jnp.exp(w_log)
        a = _sigmoid(a0_ref[...] + _dot(lo.astype(BF16), aup_ref[...]))
        gate_ref[rows, :] = _dot(_sigmoid(gl).astype(BF16), gup_ref[...])
        kk = kx * kk_ref[...]
        kk = kk / jnp.maximum(jnp.sqrt(_dot_f32(kk * kk, head_sum)), 1e-12)
        k = kx * (1.0 + (a - 1.0) * ka_ref[...])
        bonus_ref[rows, :] = _dot_f32(r * k * rk_ref[...], head_sum) * v

        cs = _dot_f32(tril, e)
        cs_end = cs[c - 1:c, :]
        p = jnp.exp(-cs)
        p_before = jnp.exp(e - cs)
        p_inv = jnp.exp(cs)
        p_rest = jnp.exp(cs - cs_end)
        be = kk * a
        at = _stack_heads(-kk * p_before, lo_half)
        rt = _stack_heads(r * p, lo_half)
        bt = _stack_heads(be * p_inv, lo_half).astype(BF16)
        kt = _stack_heads(k * p_inv, lo_half).astype(BF16)
        bh = _stack_heads(be * p_rest, lo_half).astype(BF16)
        kh = _stack_heads(k * p_rest, lo_half).astype(BF16)
        vs = _stack_heads(v, lo_half).astype(BF16)

        at16 = at.astype(BF16)
        rt16 = rt.astype(BF16)
        l_ab = jnp.where(strict, _dot_nt(at16, bt), 0.0)
        l_ak = jnp.where(strict, _dot_nt(at16, kt), 0.0)
        m_rb = jnp.where(incl, _dot_nt(rt16, bt), 0.0)
        m_rk = jnp.where(incl, _dot_nt(rt16, kt), 0.0)

        x = jnp.concatenate([at, _dot(l_ak.astype(BF16), vs)], axis=1)
        lp = l_ab.astype(BF16)
        n = 1
        while n < c:
            x = x + _dot(lp, x.astype(BF16))
            n *= 2
            if n < c:
                lp = _dot(lp, lp).astype(BF16)
        x16 = x.astype(BF16)

        mw = _dot(m_rb.astype(BF16), x16)
        qp_ref[j] = (rt + mw[:, :LANES]).astype(BF16)
        y1_ref[j] = mw[:, LANES:] + _dot(m_rk.astype(BF16), vs)
        bw = _dot_tn(bh, x16)
        p_end = jnp.exp(-cs_end)
        decay_end = jnp.where(diag, jnp.broadcast_to(p_end, (2 * c, LANES)), 0.0)
        mc_ref[j] = (decay_end + bw[:, :LANES]).astype(BF16)
        nc_ref[j] = bw[:, LANES:] + _dot_tn(kh, vs)
        return carry

    lax.fori_loop(0, ts // c, chunk, 0)


def rwkv_chunks(zc, mu, w0, w_up, a0, a_up, g_up, k_k, k_a, r_k, c_heads, ts):
    b, s, _ = zc.shape
    cw = c_heads * HEAD_DIM
    n_pairs = c_heads // 2
    c = RWKV_CHUNK
    n_chunks = s // c
    cpb = ts // c
    assert 2 * c == LANES and W_LORA + A_LORA == LANES and G_LORA == 2 * LANES
    col_lo = 3 * n_pairs
    zero = jnp.zeros((A_LORA, cw), F32)
    wup_ext = jnp.concatenate([w_up, zero], axis=0).astype(BF16)
    aup_ext = jnp.concatenate([zero, a_up], axis=0).astype(BF16)
    mu3 = mu[:3 * cw].reshape(3, cw)
    mulo = mu[3 * cw:3 * cw + LANES].reshape(1, LANES)
    mug = mu[3 * cw + LANES:].reshape(1, G_LORA)

    def tok(col):
        return pl.BlockSpec((None, ts, LANES), lambda bi, hp, t, col=col: (bi, t, col(hp)))

    def halo(col):
        return pl.BlockSpec((None, HALO, LANES),
                            lambda bi, hp, t, col=col: (bi, jnp.maximum(t * (ts // HALO) - 1, 0), col(hp)))

    cols = [lambda hp: hp, lambda hp: n_pairs + hp, lambda hp: 2 * n_pairs + hp,
            lambda hp: col_lo, lambda hp: col_lo + 1, lambda hp: col_lo + 2]

    def per_pair(rows):
        return pl.BlockSpec((rows, LANES), lambda bi, hp, t: (0, hp))

    def const(shape):
        return pl.BlockSpec(shape, lambda bi, hp, t: (0,) * len(shape))

    mat = lambda dt: jax.ShapeDtypeStruct((b, n_pairs, n_chunks, LANES, LANES), dt)
    mat_spec = pl.BlockSpec((None, None, cpb, LANES, LANES), lambda bi, hp, t: (bi, hp, t, 0, 0))
    tok_out = pl.BlockSpec((None, ts, LANES), lambda bi, hp, t: (bi, t, hp))
    return pl.pallas_call(
        functools.partial(_rwkv_chunk_kernel, ts=ts),
        grid=(b, n_pairs, s // ts),
        in_specs=[tok(cf) for cf in cols] + [halo(cf) for cf in cols] + [
            per_pair(3), const((1, LANES)), const((1, G_LORA)),
            per_pair(1), per_pair(LANES), per_pair(1), per_pair(LANES), per_pair(G_LORA),
            per_pair(1), per_pair(1), per_pair(1)],
        out_specs=[mat_spec, mat_spec, mat_spec, mat_spec, tok_out, tok_out],
        out_shape=[mat(BF16), mat(BF16), mat(F32), mat(F32),
                   jax.ShapeDtypeStruct((b, s, cw), F32), jax.ShapeDtypeStruct((b, s, cw), F32)],
        scratch_shapes=[pltpu.VMEM((ts, LANES), F32)] * 4 + [pltpu.VMEM((ts, G_LORA), F32)],
        compiler_params=_params("parallel", "parallel", "arbitrary"),
        name="rwkv_chunks",
    )(zc, zc, zc, zc, zc, zc, zc, zc, zc, zc, zc, zc,
      mu3, mulo, mug, w0.reshape(1, cw), wup_ext, a0.reshape(1, cw), aup_ext, g_up.astype(BF16),
      k_k.reshape(1, cw), k_a.reshape(1, cw), r_k.reshape(1, cw))


def _rwkv_state_kernel(mc_ref, qp_ref, nc_ref, y1_ref, bonus_ref, gate_ref, lg_ref, lb_ref, o_ref, h_ref,
                       *, n_pairs, cpb):
    c = RWKV_CHUNK

    @pl.when(pl.program_id(1) == 0)
    def _():
        h_ref[...] = jnp.zeros(h_ref.shape, F32)

    head_mean = _head_sum_matrix() * (1.0 / HEAD_DIM)

    def chunk(j, carry):
        rows = pl.ds(pl.multiple_of(j * c, c), c)
        for g in range(n_pairs):
            cols = slice(g * LANES, (g + 1) * LANES)
            h = h_ref[g]
            h16 = h.astype(BF16)
            ys = _dot(qp_ref[g, j], h16) + y1_ref[g, j]
            h_ref[g] = _dot(mc_ref[g, j], h16) + nc_ref[g, j]
            y = ys[:c, :] + ys[c:, :]
            yc = y - _dot_f32(y, head_mean)
            yn = yc * lax.rsqrt(_dot_f32(yc * yc, head_mean) + RWKV_GN_EPS)
            yn = yn * lg_ref[:, cols] + lb_ref[:, cols]
            o_ref[rows, cols] = ((yn + bonus_ref[rows, cols]) * gate_ref[rows, cols]).astype(o_ref.dtype)
        return carry

    lax.fori_loop(0, cpb, chunk, 0)


def rwkv_state_scan(mc, qp, nc, y1, bonus, gate, ln_gain, ln_bias, cpb):
    b, n_pairs, n_chunks = mc.shape[:3]
    s = n_chunks * RWKV_CHUNK
    cw = n_pairs * LANES
    ts = cpb * RWKV_CHUNK
    mat_spec = pl.BlockSpec((None, n_pairs, cpb, LANES, LANES), lambda bi, t: (bi, 0, t, 0, 0))
    tok_spec = pl.BlockSpec((None, ts, cw), lambda bi, t: (bi, t, 0))
    vec_spec = pl.BlockSpec((1, cw), lambda bi, t: (0, 0))
    return pl.pallas_call(
        functools.partial(_rwkv_state_kernel, n_pairs=n_pairs, cpb=cpb),
        grid=(b, n_chunks // cpb),
        in_specs=[mat_spec, mat_spec, mat_spec, mat_spec, tok_spec, tok_spec, vec_spec, vec_spec],
        out_specs=tok_spec,
        out_shape=jax.ShapeDtypeStruct((b, s, cw), BF16),
        scratch_shapes=[pltpu.VMEM((n_pairs, LANES, LANES), F32)],
        compiler_params=_params("parallel", "arbitrary"),
        name="rwkv_state_scan",
    )(mc, qp, nc, y1, bonus, gate, ln_gain.reshape(1, cw), ln_bias.reshape(1, cw))


def _mixer_out_kernel(oa_ref, ob_ref, oc_ref, ag_ref, wa_ref, wb_ref, wc_ref, x_ref, g_ref, o_ref):
    oa = _rms_rows(oa_ref[...], ag_ref[...]).astype(BF16)
    y = _dot(oa, wa_ref[...]) + _dot(ob_ref[...], wb_ref[...]) + _dot(oc_ref[...], wc_ref[...])
    o_ref[...] = x_ref[...] + _rms_rows(y, g_ref[...])


def mixer_out(oa, ob, oc, attn_gain, w_out, x, gain, tm):
    m, d = x.shape
    aw, bw, cw = oa.shape[1], ob.shape[1], oc.shape[1]
    w = w_out.astype(BF16)
    row = lambda width: pl.BlockSpec((tm, width), lambda i: (i, 0))
    full = lambda r, cdim: pl.BlockSpec((r, cdim), lambda i: (0, 0))
    return pl.pallas_call(
        _mixer_out_kernel,
        grid=(m // tm,),
        in_specs=[row(aw), row(bw), row(cw), full(1, aw), full(aw, d), full(bw, d), full(cw, d),
                  row(d), full(1, d)],
        out_specs=row(d),
        out_shape=jax.ShapeDtypeStruct((m, d), F32),
        compiler_params=_params("parallel"),
        name="mixer_out",
    )(oa, ob, oc, attn_gain.reshape(1, aw), w[:aw], w[aw:aw + bw], w[aw + bw:], x, gain.reshape(1, d))


def _cross_attn_kernel(x_ref, gi_ref, wq_ref, kv_ref, wo_ref, go_ref, o_ref, *, heads):
    x = x_ref[...]
    q = _dot(_rms_rows(x, gi_ref[...]).astype(BF16), wq_ref[...])
    inner = heads * MEM_HEAD_DIM
    scale = MEM_HEAD_DIM ** -0.5
    outs = []
    for h in range(heads):
        cols = slice(h * MEM_HEAD_DIM, (h + 1) * MEM_HEAD_DIM)
        s = _dot_nt(q[:, cols].astype(BF16), kv_ref[:, cols]) * scale
        m = jnp.max(s, axis=-1, keepdims=True)
        e = jnp.exp(s - m)
        p = e / jnp.sum(e, axis=-1, keepdims=True)
        outs.append(_dot(p.astype(BF16), kv_ref[:, inner + h * MEM_HEAD_DIM:inner + (h + 1) * MEM_HEAD_DIM]))
    o = jnp.concatenate(outs, axis=1).astype(BF16)
    o_ref[...] = x + _rms_rows(_dot(o, wo_ref[...]), go_ref[...])


def cross_attention(x, gain_in, wq, kv, wo, gain_out, seq, tm):
    m, d = x.shape
    inner = wq.shape[1]
    heads = inner // MEM_HEAD_DIM
    mem_tokens = kv.shape[0] // (m // seq)
    blocks_per_seq = seq // tm
    full = lambda r, cdim: pl.BlockSpec((r, cdim), lambda i: (0, 0))
    return pl.pallas_call(
        functools.partial(_cross_attn_kernel, heads=heads),
        grid=(m // tm,),
        in_specs=[pl.BlockSpec((tm, d), lambda i: (i, 0)), full(1, d), full(d, inner),
                  pl.BlockSpec((mem_tokens, 2 * inner), lambda i: (i // blocks_per_seq, 0)),
                  full(inner, d), full(1, d)],
        out_specs=pl.BlockSpec((tm, d), lambda i: (i, 0)),
        out_shape=jax.ShapeDtypeStruct((m, d), F32),
        compiler_params=_params("parallel"),
        name="cross_attention",
    )(x, gain_in.reshape(1, d), wq.astype(BF16), kv, wo.astype(BF16), gain_out.reshape(1, d))


def _conv_ffn_kernel(x_ref, xh_ref, gi_ref, wg_ref, wv_ref, cwg_ref, cwv_ref, cbg_ref, cbv_ref, wd_ref, go_ref,
                     o_ref, h_ref, acc_ref, *, tm, blocks_per_seq, sub):
    i = pl.program_id(0)
    f = pl.program_id(1)

    @pl.when(f == 0)
    def _():
        halo = _rms_rows(xh_ref[...], gi_ref[...])
        h_ref[0:HALO, :] = jnp.where(i % blocks_per_seq == 0, 0.0, halo).astype(BF16)

        def body(t, c):
            rows = pl.ds(pl.multiple_of(t * sub, sub), sub)
            h_ref[pl.ds(pl.multiple_of(HALO + t * sub, HALO), sub), :] = (
                _rms_rows(x_ref[rows, :], gi_ref[...]).astype(BF16))
            return c
        lax.fori_loop(0, tm // sub, body, 0)
        acc_ref[...] = jnp.zeros(acc_ref.shape, F32)

    h_halo = h_ref[0:HALO, :]
    h_main = h_ref[HALO:HALO + tm, :]
    row = lax.broadcasted_iota(jnp.int32, (tm, wg_ref.shape[1]), 0)

    def conv(w_ref, cw_ref, cb_ref):
        up = _dot(h_main, w_ref[...])
        up_halo = _dot(h_halo, w_ref[...])
        prev1 = jnp.where(row == 0, up_halo[HALO - 1:HALO, :], pltpu.roll(up, 1, axis=0))
        prev2 = jnp.where(row == 0, up_halo[HALO - 2:HALO - 1, :],
                          jnp.where(row == 1, up_halo[HALO - 1:HALO, :], pltpu.roll(up, 2, axis=0)))
        return cb_ref[...] + prev2 * cw_ref[0:1, :] + prev1 * cw_ref[1:2, :] + up * cw_ref[2:3, :]

    gate = conv(wg_ref, cwg_ref, cbg_ref)
    val = conv(wv_ref, cwv_ref, cbv_ref)
    act = (_gelu_tanh(gate) * val).astype(BF16)
    acc_ref[...] += _dot(act, wd_ref[...])

    @pl.when(f == pl.num_programs(1) - 1)
    def _():
        o_ref[...] = x_ref[...] + _rms_rows(acc_ref[...], go_ref[...])


def conv_ffn(x, gain_in, w_up, conv_w, conv_b, w_down, gain_out, seq, tm, tf):
    m, d = x.shape
    d_ff = w_down.shape[0]
    nf = d_ff // tf
    assert d_ff % tf == 0 and seq % tm == 0 and conv_w.shape[0] == CONV_WIDTH
    wu = w_up.astype(BF16)
    cb = conv_b.reshape(1, 2 * d_ff)
    sub = min(tm, 256)
    return pl.pallas_call(
        functools.partial(_conv_ffn_kernel, tm=tm, blocks_per_seq=seq // tm, sub=sub),
        grid=(m // tm, nf),
        in_specs=[pl.BlockSpec((tm, d), lambda i, f: (i, 0)),
                  pl.BlockSpec((HALO, d), lambda i, f: (jnp.maximum(i * (tm // HALO) - 1, 0), 0)),
                  pl.BlockSpec((1, d), lambda i, f: (0, 0)),
                  pl.BlockSpec((d, tf), lambda i, f: (0, f)),
                  pl.BlockSpec((d, tf), lambda i, f: (0, nf + f)),
                  pl.BlockSpec((CONV_WIDTH, tf), lambda i, f: (0, f)),
                  pl.BlockSpec((CONV_WIDTH, tf), lambda i, f: (0, nf + f)),
                  pl.BlockSpec((1, tf), lambda i, f: (0, f)),
                  pl.BlockSpec((1, tf), lambda i, f: (0, nf + f)),
                  pl.BlockSpec((tf, d), lambda i, f: (f, 0)),
                  pl.BlockSpec((1, d), lambda i, f: (0, 0))],
        out_specs=pl.BlockSpec((tm, d), lambda i, f: (i, 0)),
        out_shape=jax.ShapeDtypeStruct((m, d), F32),
        scratch_shapes=[pltpu.VMEM((tm + HALO, d), BF16), pltpu.VMEM((tm, d), F32)],
        compiler_params=_params("parallel", "arbitrary"),
        name="conv_ffn",
    )(x, x, gain_in.reshape(1, d), wu, wu, conv_w, conv_w, cb, cb, w_down.astype(BF16), gain_out.reshape(1, d))


def _tile_sizes(seq):
    tm = 512 if seq % 512 == 0 else seq
    return dict(tm_proj=min(1024, seq), tm=tm, tf=512, ts=min(512, seq), cpb=min(8, seq // RWKV_CHUNK))


def kernel(x, mem, rel_bias_table, sandwich_gains, mem_src_gain, w_in, w_out, attn_out_gain, sgu_norm_gain, sgu_w, sgu_b, sgu_out_gain, rwkv_mu, rwkv_w0, rwkv_w_up, rwkv_a0, rwkv_a_up, rwkv_g_up, rwkv_k_k, rwkv_k_a, rwkv_r_k, rwkv_ln_gain, rwkv_ln_bias, mem_wq, mem_wkv, mem_wo, ffn_w_up, ffn_conv_w, ffn_conv_b, ffn_w_down):
    b, s, d = x.shape
    depth = w_in.shape[0]
    a_heads = rel_bias_table.shape[1]
    aw = a_heads * HEAD_DIM
    bw = sgu_norm_gain.shape[1]
    cw = rwkv_w0.shape[1]
    c_heads = cw // HEAD_DIM
    t = _tile_sizes(s)
    m = b * s
    xf = x.reshape(m, d)
    memf = mem.reshape(b * mem.shape[1], d)
    for l in range(depth):
        g = sandwich_gains[l]
        wi = w_in[l].astype(BF16)
        za = rms_matmul(xf, g[0], wi[:, :3 * aw], t["tm_proj"], 3 * aw // 3)
        zb = rms_matmul(xf, g[0], wi[:, 3 * aw:3 * aw + 2 * bw], t["tm_proj"], bw)
        zc = rms_matmul(xf, g[0], wi[:, 3 * aw + 2 * bw:], t["tm_proj"], (wi.shape[1] - 3 * aw - 2 * bw) // 3)
        oa = dilated_attention(za.reshape(b, s, 3 * aw), rel_bias_table, a_heads)
        ob = spatial_gating(zb, sgu_norm_gain[l], sgu_w[l], sgu_b[l], sgu_out_gain[l], t["tm"])
        mc, qp, nc, y1, bonus, gate = rwkv_chunks(
            zc.reshape(b, s, -1), rwkv_mu[l], rwkv_w0[l], rwkv_w_up[l], rwkv_a0[l], rwkv_a_up[l], rwkv_g_up[l],
            rwkv_k_k[l], rwkv_k_a[l], rwkv_r_k[l], c_heads, t["ts"])
        oc = rwkv_state_scan(mc, qp, nc, y1, bonus, gate, rwkv_ln_gain[l], rwkv_ln_bias[l], t["cpb"])
        xf = mixer_out(oa.reshape(m, aw), ob, oc.reshape(m, cw), attn_out_gain[l], w_out[l], xf, g[1], t["tm"])
        kv = rms_matmul(memf, mem_src_gain[l], mem_wkv[l].astype(BF16), memf.shape[0], mem_wkv.shape[2] // 2,
                        out_dtype=BF16)
        xf = cross_attention(xf, g[2], mem_wq[l], kv, mem_wo[l], g[3], s, t["tm"])
        xf = conv_ffn(xf, g[4], ffn_w_up[l], ffn_conv_w[l], ffn_conv_b[l], ffn_w_down[l], g[5], s, t["tm"], t["tf"])
    return xf.reshape(b, s, d)
```

```python
import functools
import itertools

import numpy as np
import jax
import jax.numpy as jnp
from jax import lax
from jax.experimental import pallas as pl
from jax.experimental.pallas import tpu as pltpu

HEAD_DIM = 64
LANES = 128
DILATED_PATTERNS = ((128, 1), (512, 4), (2048, 16))
BLOCK = 128
REL_BUCKETS = 32
REL_MAX_DISTANCE = 2048
CHUNK = 128
RWKV_CHUNK = 64
W_LORA = 64
A_LORA = 64
G_LORA = 256
RWKV_GN_EPS = 64e-5
MEM_HEAD_DIM = 128
CONV_WIDTH = 3
NORM_EPS = 1e-6
NEG_INF = -1e30
HALO = 8

F32 = jnp.float32
BF16 = jnp.bfloat16
VMEM_LIMIT_BYTES = 56 * 1024 * 1024


def _params(*sem):
    return pltpu.CompilerParams(dimension_semantics=sem, vmem_limit_bytes=VMEM_LIMIT_BYTES)


def _dot(a, b):
    return jnp.dot(a, b, preferred_element_type=F32)


def _dot_nt(a, b):
    return lax.dot_general(a, b, (((1,), (1,)), ((), ())), preferred_element_type=F32)


def _dot_tn(a, b):
    return lax.dot_general(a, b, (((0,), (0,)), ((), ())), preferred_element_type=F32)


def _rms_rows(x, gain):
    ms = jnp.mean(x * x, axis=-1, keepdims=True)
    return x * lax.rsqrt(ms + NORM_EPS) * gain


def _gelu_tanh(x):
    return 0.5 * x * (1.0 + jnp.tanh(np.sqrt(2.0 / np.pi).astype(np.float32) * (x + 0.044715 * (x * x * x))))


def _sigmoid(x):
    return 1.0 / (1.0 + jnp.exp(-x))


def _rms_matmul_kernel(x_ref, g_ref, w_ref, o_ref, h_ref, *, sub):
    @pl.when(pl.program_id(1) == 0)
    def _():
        def body(i, c):
            rows = pl.ds(pl.multiple_of(i * sub, sub), sub)
            h_ref[rows, :] = _rms_rows(x_ref[rows, :], g_ref[...]).astype(BF16)
            return c
        lax.fori_loop(0, x_ref.shape[0] // sub, body, 0)

    o_ref[...] = _dot(h_ref[...], w_ref[...]).astype(o_ref.dtype)


def rms_matmul(x, gain, w, tm, tn, out_dtype=F32):
    m, k = x.shape
    n = w.shape[1]
    assert m % tm == 0 and n % tn == 0
    sub = min(tm, 256)
    return pl.pallas_call(
        functools.partial(_rms_matmul_kernel, sub=sub),
        grid=(m // tm, n // tn),
        in_specs=[pl.BlockSpec((tm, k), lambda i, j: (i, 0)),
                  pl.BlockSpec((1, k), lambda i, j: (0, 0)),
                  pl.BlockSpec((k, tn), lambda i, j: (0, j))],
        out_specs=pl.BlockSpec((tm, tn), lambda i, j: (i, j)),
        out_shape=jax.ShapeDtypeStruct((m, n), out_dtype),
        scratch_shapes=[pltpu.VMEM((tm, k), BF16)],
        compiler_params=_params("parallel", "arbitrary"),
        name="rms_matmul",
    )(x, gain.reshape(1, k), w)


def _t5_causal_bucket(dist):
    max_exact = REL_BUCKETS // 2
    d = np.maximum(dist, 0)
    scaled = np.log(np.maximum(d, 1) / max_exact) / np.log(REL_MAX_DISTANCE / max_exact)
    large = np.minimum(max_exact + (scaled * (REL_BUCKETS - max_exact)).astype(np.int32), REL_BUCKETS - 1)
    return np.where(d < max_exact, d, large).astype(np.int32)


def _bucket_tables():
    out = []
    qi = np.arange(BLOCK)[:, None]
    kj = np.arange(2 * BLOCK)[None, :]
    for window, dilation in DILATED_PATTERNS:
        steps = window // dilation
        assert steps <= BLOCK
        delta = qi + BLOCK - kj
        band = (delta >= 0) & (delta <= steps)
        bucket = _t5_causal_bucket(np.clip(delta, 0, steps) * dilation)
        out.append(np.where(band, bucket, -1))
    return np.stack(out).astype(np.int32)


def _dilated_attn_kernel(table_ref, bucket_ref, q_ref, k_ref, v_ref, o_ref, bias_ref, acc_ref, lse_ref,
                         *, seq, n_pairs, unroll):
    hp = pl.program_id(0)
    lane = lax.broadcasted_iota(jnp.int32, (BLOCK, LANES), 1)
    lo_half = lane < HEAD_DIM

    @pl.when(pl.program_id(1) == 0)
    def _():
        key = lax.broadcasted_iota(jnp.int32, (BLOCK, 2 * BLOCK), 1)
        for p in range(len(DILATED_PATTERNS)):
            bucket = bucket_ref[p]
            for h in range(2):
                val = jnp.zeros(bucket.shape, F32)
                for b in range(REL_BUCKETS):
                    val = jnp.where(bucket == b, table_ref[b, 2 * hp + h], val)
                val = jnp.where(bucket < 0, NEG_INF, val)
                bias_ref[p, h, 0] = val
                bias_ref[p, h, 1] = jnp.where(key < BLOCK, NEG_INF, val)

    scale = HEAD_DIM ** -0.5
    for p, (window, d) in enumerate(DILATED_PATTERNS):
        n_blk = seq // (d * BLOCK)
        shift = int(np.log2(d))

        def block(idx, p=p, d=d, shift=shift):
            r = idx & (d - 1)
            n = idx >> shift
            base = r + n * (BLOCK * d)
            prev = r + jnp.maximum(n - 1, 0) * (BLOCK * d)
            if d == 1:
                cur_rows = pl.ds(pl.multiple_of(base, BLOCK), BLOCK)
                prev_rows = pl.ds(pl.multiple_of(prev, BLOCK), BLOCK)
            else:
                cur_rows = pl.ds(base, BLOCK, stride=d)
                prev_rows = pl.ds(prev, BLOCK, stride=d)
            q = q_ref[cur_rows, :] * scale
            kcat = jnp.concatenate([k_ref[prev_rows, :], k_ref[cur_rows, :]], axis=0).astype(BF16)
            vcat = jnp.concatenate([v_ref[prev_rows, :], v_ref[cur_rows, :]], axis=0).astype(BF16)
            var = jnp.where(n == 0, 1, 0)
            scores = [_dot_nt(jnp.where(lo_half if h == 0 else ~lo_half, q, 0.0).astype(BF16), kcat)
                      for h in range(2)]
            yield
            outs, lses, dens = [], [], []
            for h in range(2):
                s = scores[h] + bias_ref[p, h, var]
                m = jnp.max(s, axis=-1, keepdims=True)
                e = jnp.exp(s - m)
                den = jnp.sum(e, axis=-1, keepdims=True)
                outs.append(_dot(e.astype(BF16), vcat))
                dens.append(den)
                lses.append(m + jnp.log(den))
            yield
            acc_ref[p, cur_rows, :] = jnp.where(lo_half, outs[0] * (1.0 / dens[0]), outs[1] * (1.0 / dens[1]))
            lse_ref[p, cur_rows, :] = jnp.where(lo_half, lses[0], lses[1])

        def block_group(i, carry, block=block):
            stages = [block(i * unroll + u) for u in range(unroll)]
            for _ in itertools.zip_longest(*stages):
                pass
            return carry

        lax.fori_loop(0, d * n_blk // unroll, block_group, 0)

    def merge(i, carry):
        rows = pl.ds(pl.multiple_of(i * BLOCK, BLOCK), BLOCK)
        l0, l1, l2 = lse_ref[0, rows, :], lse_ref[1, rows, :], lse_ref[2, rows, :]
        mx = jnp.maximum(jnp.maximum(l0, l1), l2)
        w0, w1, w2 = jnp.exp(l0 - mx), jnp.exp(l1 - mx), jnp.exp(l2 - mx)
        num = w0 * acc_ref[0, rows, :] + w1 * acc_ref[1, rows, :] + w2 * acc_ref[2, rows, :]
        o_ref[rows, :] = num / (w0 + w1 + w2)
        return carry

    lax.fori_loop(0, seq // BLOCK, merge, 0)


def dilated_attention(za, rel_bias_table, a_heads):
    b, s, _ = za.shape
    n_pairs = a_heads // 2
    unroll = 4
    assert s % (DILATED_PATTERNS[-1][1] * BLOCK) == 0 and (s // BLOCK) % unroll == 0
    buckets = jnp.asarray(_bucket_tables())
    n_pat = len(DILATED_PATTERNS)
    return pl.pallas_call(
        functools.partial(_dilated_attn_kernel, seq=s, n_pairs=n_pairs, unroll=unroll),
        grid=(n_pairs, b),
        in_specs=[pl.BlockSpec(memory_space=pltpu.SMEM),
                  pl.BlockSpec((n_pat, BLOCK, 2 * BLOCK), lambda hp, bi: (0, 0, 0)),
                  pl.BlockSpec((None, s, LANES), lambda hp, bi: (bi, 0, hp)),
                  pl.BlockSpec((None, s, LANES), lambda hp, bi: (bi, 0, n_pairs + hp)),
                  pl.BlockSpec((None, s, LANES), lambda hp, bi: (bi, 0, 2 * n_pairs + hp))],
        out_specs=pl.BlockSpec((None, s, LANES), lambda hp, bi: (bi, 0, hp)),
        out_shape=jax.ShapeDtypeStruct((b, s, a_heads * HEAD_DIM), F32),
        scratch_shapes=[pltpu.VMEM((n_pat, 2, 2, BLOCK, 2 * BLOCK), F32),
                        pltpu.VMEM((n_pat, s, LANES), F32),
                        pltpu.VMEM((n_pat, s, LANES), F32)],
        compiler_params=_params("parallel", "arbitrary"),
        name="dilated_attention",
    )(rel_bias_table, buckets, za, za, za)


def _sgu_kernel(z_ref, ng_ref, w_ref, b_ref, og_ref, o_ref, *, width, groups):
    z = _gelu_tanh(z_ref[...])
    u = z[:, :width]
    g = z[:, width:]
    gc = g - jnp.mean(g, axis=-1, keepdims=True)
    gn = gc * lax.rsqrt(jnp.mean(gc * gc, axis=-1, keepdims=True) + NORM_EPS) * ng_ref[...]
    ri = lax.broadcasted_iota(jnp.int32, (CHUNK, CHUNK), 0)
    ci = lax.broadcasted_iota(jnp.int32, (CHUNK, CHUNK), 1)
    causal = ri >= ci
    lane = lax.broadcasted_iota(jnp.int32, (CHUNK, LANES), 1)
    lo_half = lane < HEAD_DIM
    rows = z.shape[0]
    cols = []
    for j in range(groups // 2):
        w0 = jnp.where(causal, w_ref[2 * j], 0.0).astype(BF16)
        w1 = jnp.where(causal, w_ref[2 * j + 1], 0.0).astype(BF16)
        parts = []
        for c in range(rows // CHUNK):
            blk = gn[c * CHUNK:(c + 1) * CHUNK, j * LANES:(j + 1) * LANES]
            g0 = jnp.where(lo_half, blk, 0.0).astype(BF16)
            g1 = jnp.where(lo_half, 0.0, blk).astype(BF16)
            parts.append(_dot(w0, g0) + _dot(w1, g1) + b_ref[:, j * LANES:(j + 1) * LANES])
        cols.append(jnp.concatenate(parts, axis=0))
    mixed = jnp.concatenate(cols, axis=1)
    o_ref[...] = _rms_rows(u * mixed, og_ref[...]).astype(o_ref.dtype)


def spatial_gating(zb, norm_gain, w_s, b_s, out_gain, tm):
    m, two_w = zb.shape
    width = two_w // 2
    groups = w_s.shape[0]
    bias = jnp.repeat(b_s.T, HEAD_DIM, axis=1)
    return pl.pallas_call(
        functools.partial(_sgu_kernel, width=width, groups=groups),
        grid=(m // tm,),
        in_specs=[pl.BlockSpec((tm, two_w), lambda i: (i, 0)),
                  pl.BlockSpec((1, width), lambda i: (0, 0)),
                  pl.BlockSpec((groups, CHUNK, CHUNK), lambda i: (0, 0, 0)),
                  pl.BlockSpec((CHUNK, width), lambda i: (0, 0)),
                  pl.BlockSpec((1, width), lambda i: (0, 0))],
        out_specs=pl.BlockSpec((tm, width), lambda i: (i, 0)),
        out_shape=jax.ShapeDtypeStruct((m, width), BF16),
        compiler_params=_params("parallel"),
        name="spatial_gating",
    )(zb, norm_gain.reshape(1, width), w_s, bias, out_gain.reshape(1, width))


def _head_sums(z, lo_half):
    s0 = jnp.sum(jnp.where(lo_half, z, 0.0), axis=-1, keepdims=True)
    s1 = jnp.sum(jnp.where(lo_half, 0.0, z), axis=-1, keepdims=True)
    return jnp.where(lo_half, s0, s1)


def _stack_heads(z, lo_half):
    return jnp.concatenate([jnp.where(lo_half, z, 0.0), jnp.where(lo_half, 0.0, z)], axis=0)


def _rwkv_chunk_kernel(r_ref, k_ref, v_ref, lo_ref, g0_ref, g1_ref,
                       rp_ref, kp_ref, vp_ref, lop_ref, g0p_ref, g1p_ref,
                       mu_ref, mulo_ref, mug_ref, w0_ref, wup_ref, a0_ref, aup_ref, gup_ref,
                       kk_ref, ka_ref, rk_ref,
                       mc_ref, qp_ref, nc_ref, y1_ref, bonus_ref, gate_ref,
                       rs_ref, ks_ref, vs_ref, los_ref, gs_ref, *, ts, unroll):
    c = RWKV_CHUNK
    first = pl.program_id(2) == 0
    row = lax.broadcasted_iota(jnp.int32, (ts, LANES), 0)

    def shifted(x_ref, halo_ref, mu):
        x = x_ref[...]
        before = jnp.where(first, 0.0, halo_ref[HALO - 1:HALO, :])
        prev = jnp.where(row == 0, before, pltpu.roll(x, 1, axis=0))
        return x + (prev - x) * mu

    rs_ref[...] = shifted(r_ref, rp_ref, mu_ref[0:1, :])
    ks_ref[...] = shifted(k_ref, kp_ref, mu_ref[1:2, :])
    vs_ref[...] = shifted(v_ref, vp_ref, mu_ref[2:3, :])
    los_ref[...] = shifted(lo_ref, lop_ref, mulo_ref[...])
    gs_ref[:, :LANES] = shifted(g0_ref, g0p_ref, mug_ref[:, :LANES])
    gs_ref[:, LANES:] = shifted(g1_ref, g1p_ref, mug_ref[:, LANES:])

    lane = lax.broadcasted_iota(jnp.int32, (c, LANES), 1)
    lo_half = lane < HEAD_DIM
    trow = lax.broadcasted_iota(jnp.int32, (c, LANES), 0)
    ri = lax.broadcasted_iota(jnp.int32, (2 * c, 2 * c), 0)
    ci = lax.broadcasted_iota(jnp.int32, (2 * c, 2 * c), 1)
    same = (ri // c) == (ci // c)
    strict = same & ((ri % c) > (ci % c))
    incl = same & ((ri % c) >= (ci % c))
    diag = ri == ci

    def chunk(j):
        rows = pl.ds(pl.multiple_of(j * c, c), c)
        r = rs_ref[rows, :]
        kx = ks_ref[rows, :]
        v = vs_ref[rows, :]
        lo = los_ref[rows, :]
        gl = gs_ref[rows, :]

        z = w0_ref[...] + _dot(jnp.tanh(lo).astype(BF16), wup_ref[...])
        w_log = -(jnp.maximum(-z, 0.0) + jnp.log(1.0 + jnp.exp(-jnp.abs(z)))) - 0.5
        e = jnp.exp(w_log)
        a = _sigmoid(a0_ref[...] + _dot(lo.astype(BF16), aup_ref[...]))
        gate_ref[rows, :] = _dot(_sigmoid(gl).astype(BF16), gup_ref[...])
        yield
        kk = kx * kk_ref[...]
        kk = kk / jnp.maximum(jnp.sqrt(_head_sums(kk * kk, lo_half)), 1e-12)
        k = kx * (1.0 + (a - 1.0) * ka_ref[...])
        bonus_ref[rows, :] = _head_sums(r * k * rk_ref[...], lo_half) * v

        cs = e
        for sh in (1, 2, 4, 8, 16, 32):
            cs = cs + jnp.where(trow >= sh, pltpu.roll(cs, sh, axis=0), 0.0)
        cs_end = cs[c - 1:c, :]
        p = jnp.exp(-cs)
        p_before = jnp.exp(e - cs)
        p_inv = jnp.exp(cs)
        p_rest = jnp.exp(cs - cs_end)
        be = kk * a
        at = _stack_heads(-kk * p_before, lo_half)
        rt = _stack_heads(r * p, lo_half)
        bt = _stack_heads(be * p_inv, lo_half).astype(BF16)
        kt = _stack_heads(k * p_inv, lo_half).astype(BF16)
        bh = _stack_heads(be * p_rest, lo_half).astype(BF16)
        kh = _stack_heads(k * p_rest, lo_half).astype(BF16)
        vs = _stack_heads(v, lo_half).astype(BF16)

        cc = _dot_nt(jnp.concatenate([at, rt], axis=0).astype(BF16), jnp.concatenate([bt, kt], axis=0))
        yield
        l_ab = jnp.where(strict, cc[:2 * c, :2 * c], 0.0)
        l_ak = jnp.where(strict, cc[:2 * c, 2 * c:], 0.0)
        m_rb = jnp.where(incl, cc[2 * c:, :2 * c], 0.0)
        m_rk = jnp.where(incl, cc[2 * c:, 2 * c:], 0.0)

        x = jnp.concatenate([at, _dot(l_ak.astype(BF16), vs)], axis=1)
        yield
        lp = l_ab.astype(BF16)
        n = 1
        while n < c:
            x = x + _dot(lp, x.astype(BF16))
            yield
            n *= 2
            if n < c:
                lp = _dot(lp, lp).astype(BF16)
        x16 = x.astype(BF16)

        mw = _dot(m_rb.astype(BF16), x16)
        yield
        qp_ref[j] = (rt + mw[:, :LANES]).astype(BF16)
        y1_ref[j] = mw[:, LANES:] + _dot(m_rk.astype(BF16), vs)
        bw = _dot_tn(bh, x16)
        p_end = jnp.exp(-cs_end)
        decay_end = jnp.where(diag, jnp.broadcast_to(p_end, (2 * c, LANES)), 0.0)
        mc_ref[j] = (decay_end + bw[:, :LANES]).astype(BF16)
        nc_ref[j] = bw[:, LANES:] + _dot_tn(kh, vs)

    def chunk_group(i, carry):
        stages = [chunk(i * unroll + u) for u in range(unroll)]
        for _ in itertools.zip_longest(*stages):
            pass
        return carry

    lax.fori_loop(0, ts // (c * unroll), chunk_group, 0)


def rwkv_chunks(zc, mu, w0, w_up, a0, a_up, g_up, k_k, k_a, r_k, c_heads, ts):
    b, s, _ = zc.shape
    cw = c_heads * HEAD_DIM
    n_pairs = c_heads // 2
    c = RWKV_CHUNK
    n_chunks = s // c
    cpb = ts // c
    assert 2 * c == LANES and W_LORA + A_LORA == LANES and G_LORA == 2 * LANES
    col_lo = 3 * n_pairs
    zero = jnp.zeros((A_LORA, cw), F32)
    wup_ext = jnp.concatenate([w_up, zero], axis=0).astype(BF16)
    aup_ext = jnp.concatenate([zero, a_up], axis=0).astype(BF16)
    mu3 = mu[:3 * cw].reshape(3, cw)
    mulo = mu[3 * cw:3 * cw + LANES].reshape(1, LANES)
    mug = mu[3 * cw + LANES:].reshape(1, G_LORA)

    def tok(col):
        return pl.BlockSpec((None, ts, LANES), lambda bi, hp, t, col=col: (bi, t, col(hp)))

    def halo(col):
        return pl.BlockSpec((None, HALO, LANES),
                            lambda bi, hp, t, col=col: (bi, jnp.maximum(t * (ts // HALO) - 1, 0), col(hp)))

    cols = [lambda hp: hp, lambda hp: n_pairs + hp, lambda hp: 2 * n_pairs + hp,
            lambda hp: col_lo, lambda hp: col_lo + 1, lambda hp: col_lo + 2]

    def per_pair(rows):
        return pl.BlockSpec((rows, LANES), lambda bi, hp, t: (0, hp))

    def const(shape):
        return pl.BlockSpec(shape, lambda bi, hp, t: (0,) * len(shape))

    mat = lambda dt: jax.ShapeDtypeStruct((b, n_pairs, n_chunks, LANES, LANES), dt)
    mat_spec = pl.BlockSpec((None, None, cpb, LANES, LANES), lambda bi, hp, t: (bi, hp, t, 0, 0))
    tok_out = pl.BlockSpec((None, ts, LANES), lambda bi, hp, t: (bi, t, hp))
    return pl.pallas_call(
        functools.partial(_rwkv_chunk_kernel, ts=ts, unroll=min(8, cpb)),
        grid=(b, n_pairs, s // ts),
        in_specs=[tok(cf) for cf in cols] + [halo(cf) for cf in cols] + [
            per_pair(3), const((1, LANES)), const((1, G_LORA)),
            per_pair(1), per_pair(LANES), per_pair(1), per_pair(LANES), per_pair(G_LORA),
            per_pair(1), per_pair(1), per_pair(1)],
        out_specs=[mat_spec, mat_spec, mat_spec, mat_spec, tok_out, tok_out],
        out_shape=[mat(BF16), mat(BF16), mat(F32), mat(F32),
                   jax.ShapeDtypeStruct((b, s, cw), F32), jax.ShapeDtypeStruct((b, s, cw), F32)],
        scratch_shapes=[pltpu.VMEM((ts, LANES), F32)] * 4 + [pltpu.VMEM((ts, G_LORA), F32)],
        compiler_params=_params("parallel", "parallel", "arbitrary"),
        name="rwkv_chunks",
    )(zc, zc, zc, zc, zc, zc, zc, zc, zc, zc, zc, zc,
      mu3, mulo, mug, w0.reshape(1, cw), wup_ext, a0.reshape(1, cw), aup_ext, g_up.astype(BF16),
      k_k.reshape(1, cw), k_a.reshape(1, cw), r_k.reshape(1, cw))


def _rwkv_state_kernel(mc_ref, qp_ref, nc_ref, y1_ref, bonus_ref, gate_ref, lg_ref, lb_ref, o_ref, h_ref,
                       *, n_pairs, cpb):
    c = RWKV_CHUNK

    @pl.when(pl.program_id(1) == 0)
    def _():
        h_ref[...] = jnp.zeros(h_ref.shape, F32)

    lane = lax.broadcasted_iota(jnp.int32, (c, LANES), 1)
    lo_half = lane < HEAD_DIM
    inv_n = 1.0 / HEAD_DIM

    def pair(g, j, rows):
        cols = slice(g * LANES, (g + 1) * LANES)
        h16 = h_ref[g].astype(BF16)
        ys = _dot(qp_ref[g, j], h16) + y1_ref[g, j]
        h_ref[g] = _dot(mc_ref[g, j], h16) + nc_ref[g, j]
        yield
        y = ys[:c, :] + ys[c:, :]
        yc = y - _head_sums(y, lo_half) * inv_n
        yn = yc * lax.rsqrt(_head_sums(yc * yc, lo_half) * inv_n + RWKV_GN_EPS)
        yn = yn * lg_ref[:, cols] + lb_ref[:, cols]
        o_ref[rows, cols] = ((yn + bonus_ref[rows, cols]) * gate_ref[rows, cols]).astype(o_ref.dtype)

    def chunk(j, carry):
        rows = pl.ds(pl.multiple_of(j * c, c), c)
        stages = [pair(g, j, rows) for g in range(n_pairs)]
        for _ in itertools.zip_longest(*stages):
            pass
        return carry

    lax.fori_loop(0, cpb, chunk, 0)


def rwkv_state_scan(mc, qp, nc, y1, bonus, gate, ln_gain, ln_bias, cpb):
    b, n_pairs, n_chunks = mc.shape[:3]
    s = n_chunks * RWKV_CHUNK
    cw = n_pairs * LANES
    ts = cpb * RWKV_CHUNK
    mat_spec = pl.BlockSpec((None, n_pairs, cpb, LANES, LANES), lambda bi, t: (bi, 0, t, 0, 0))
    tok_spec = pl.BlockSpec((None, ts, cw), lambda bi, t: (bi, t, 0))
    vec_spec = pl.BlockSpec((1, cw), lambda bi, t: (0, 0))
    return pl.pallas_call(
        functools.partial(_rwkv_state_kernel, n_pairs=n_pairs, cpb=cpb),
        grid=(b, n_chunks // cpb),
        in_specs=[mat_spec, mat_spec, mat_spec, mat_spec, tok_spec, tok_spec, vec_spec, vec_spec],
        out_specs=tok_spec,
        out_shape=jax.ShapeDtypeStruct((b, s, cw), BF16),
        scratch_shapes=[pltpu.VMEM((n_pairs, LANES, LANES), F32)],
        compiler_params=_params("parallel", "arbitrary"),
        name="rwkv_state_scan",
    )(mc, qp, nc, y1, bonus, gate, ln_gain.reshape(1, cw), ln_bias.reshape(1, cw))


def _mixer_out_kernel(oa_ref, ob_ref, oc_ref, ag_ref, wa_ref, wb_ref, wc_ref, x_ref, g_ref, o_ref):
    oa = _rms_rows(oa_ref[...], ag_ref[...]).astype(BF16)
    y = _dot(oa, wa_ref[...]) + _dot(ob_ref[...], wb_ref[...]) + _dot(oc_ref[...], wc_ref[...])
    o_ref[...] = x_ref[...] + _rms_rows(y, g_ref[...])


def mixer_out(oa, ob, oc, attn_gain, w_out, x, gain, tm):
    m, d = x.shape
    aw, bw, cw = oa.shape[1], ob.shape[1], oc.shape[1]
    w = w_out.astype(BF16)
    row = lambda width: pl.BlockSpec((tm, width), lambda i: (i, 0))
    full = lambda r, cdim: pl.BlockSpec((r, cdim), lambda i: (0, 0))
    return pl.pallas_call(
        _mixer_out_kernel,
        grid=(m // tm,),
        in_specs=[row(aw), row(bw), row(cw), full(1, aw), full(aw, d), full(bw, d), full(cw, d),
                  row(d), full(1, d)],
        out_specs=row(d),
        out_shape=jax.ShapeDtypeStruct((m, d), F32),
        compiler_params=_params("parallel"),
        name="mixer_out",
    )(oa, ob, oc, attn_gain.reshape(1, aw), w[:aw], w[aw:aw + bw], w[aw + bw:], x, gain.reshape(1, d))


def _cross_attn_kernel(x_ref, gi_ref, wq_ref, kv_ref, wo_ref, go_ref, o_ref, *, heads):
    x = x_ref[...]
    q = _dot(_rms_rows(x, gi_ref[...]).astype(BF16), wq_ref[...])
    inner = heads * MEM_HEAD_DIM
    scale = MEM_HEAD_DIM ** -0.5
    outs = []
    for h in range(heads):
        cols = slice(h * MEM_HEAD_DIM, (h + 1) * MEM_HEAD_DIM)
        s = _dot_nt(q[:, cols].astype(BF16), kv_ref[:, cols]) * scale
        m = jnp.max(s, axis=-1, keepdims=True)
        e = jnp.exp(s - m)
        p = e / jnp.sum(e, axis=-1, keepdims=True)
        outs.append(_dot(p.astype(BF16), kv_ref[:, inner + h * MEM_HEAD_DIM:inner + (h + 1) * MEM_HEAD_DIM]))
    o = jnp.concatenate(outs, axis=1).astype(BF16)
    o_ref[...] = x + _rms_rows(_dot(o, wo_ref[...]), go_ref[...])


def cross_attention(x, gain_in, wq, kv, wo, gain_out, seq, tm):
    m, d = x.shape
    inner = wq.shape[1]
    heads = inner // MEM_HEAD_DIM
    mem_tokens = kv.shape[0] // (m // seq)
    blocks_per_seq = seq // tm
    full = lambda r, cdim: pl.BlockSpec((r, cdim), lambda i: (0, 0))
    return pl.pallas_call(
        functools.partial(_cross_attn_kernel, heads=heads),
        grid=(m // tm,),
        in_specs=[pl.BlockSpec((tm, d), lambda i: (i, 0)), full(1, d), full(d, inner),
                  pl.BlockSpec((mem_tokens, 2 * inner), lambda i: (i // blocks_per_seq, 0)),
                  full(inner, d), full(1, d)],
        out_specs=pl.BlockSpec((tm, d), lambda i: (i, 0)),
        out_shape=jax.ShapeDtypeStruct((m, d), F32),
        compiler_params=_params("parallel"),
        name="cross_attention",
    )(x, gain_in.reshape(1, d), wq.astype(BF16), kv, wo.astype(BF16), gain_out.reshape(1, d))


def _conv_ffn_kernel(x_ref, xh_ref, gi_ref, wg_ref, wv_ref, cwg_ref, cwv_ref, cbg_ref, cbv_ref, wd_ref, go_ref,
                     o_ref, h_ref, acc_ref, *, tm, blocks_per_seq, sub):
    i = pl.program_id(0)
    f = pl.program_id(1)

    @pl.when(f == 0)
    def _():
        halo = _rms_rows(xh_ref[...], gi_ref[...])
        h_ref[0:HALO, :] = jnp.where(i % blocks_per_seq == 0, 0.0, halo).astype(BF16)

        def body(t, c):
            rows = pl.ds(pl.multiple_of(t * sub, sub), sub)
            h_ref[pl.ds(pl.multiple_of(HALO + t * sub, HALO), sub), :] = (
                _rms_rows(x_ref[rows, :], gi_ref[...]).astype(BF16))
            return c
        lax.fori_loop(0, tm // sub, body, 0)
        acc_ref[...] = jnp.zeros(acc_ref.shape, F32)

    h_halo = h_ref[0:HALO, :]
    h_main = h_ref[HALO:HALO + tm, :]
    row = lax.broadcasted_iota(jnp.int32, (tm, wg_ref.shape[1]), 0)

    def conv(w_ref, cw_ref, cb_ref):
        up = _dot(h_main, w_ref[...])
        up_halo = _dot(h_halo, w_ref[...])
        prev1 = jnp.where(row == 0, up_halo[HALO - 1:HALO, :], pltpu.roll(up, 1, axis=0))
        prev2 = jnp.where(row == 0, up_halo[HALO - 2:HALO - 1, :],
                          jnp.where(row == 1, up_halo[HALO - 1:HALO, :], pltpu.roll(up, 2, axis=0)))
        return cb_ref[...] + prev2 * cw_ref[0:1, :] + prev1 * cw_ref[1:2, :] + up * cw_ref[2:3, :]

    gate = conv(wg_ref, cwg_ref, cbg_ref)
    val = conv(wv_ref, cwv_ref, cbv_ref)
    act = (_gelu_tanh(gate) * val).astype(BF16)
    acc_ref[...] += _dot(act, wd_ref[...])

    @pl.when(f == pl.num_programs(1) - 1)
    def _():
        o_ref[...] = x_ref[...] + _rms_rows(acc_ref[...], go_ref[...])


def conv_ffn(x, gain_in, w_up, conv_w, conv_b, w_down, gain_out, seq, tm, tf):
    m, d = x.shape
    d_ff = w_down.shape[0]
    nf = d_ff // tf
    assert d_ff % tf == 0 and seq % tm == 0 and conv_w.shape[0] == CONV_WIDTH
    wu = w_up.astype(BF16)
    cb = conv_b.reshape(1, 2 * d_ff)
    sub = min(tm, 256)
    return pl.pallas_call(
        functools.partial(_conv_ffn_kernel, tm=tm, blocks_per_seq=seq // tm, sub=sub),
        grid=(m // tm, nf),
        in_specs=[pl.BlockSpec((tm, d), lambda i, f: (i, 0)),
                  pl.BlockSpec((HALO, d), lambda i, f: (jnp.maximum(i * (tm // HALO) - 1, 0), 0)),
                  pl.BlockSpec((1, d), lambda i, f: (0, 0)),
                  pl.BlockSpec((d, tf), lambda i, f: (0, f)),
                  pl.BlockSpec((d, tf), lambda i, f: (0, nf + f)),
                  pl.BlockSpec((CONV_WIDTH, tf), lambda i, f: (0, f)),
                  pl.BlockSpec((CONV_WIDTH, tf), lambda i, f: (0, nf + f)),
                  pl.BlockSpec((1, tf), lambda i, f: (0, f)),
                  pl.BlockSpec((1, tf), lambda i, f: (0, nf + f)),
                  pl.BlockSpec((tf, d), lambda i, f: (f, 0)),
                  pl.BlockSpec((1, d), lambda i, f: (0, 0))],
        out_specs=pl.BlockSpec((tm, d), lambda i, f: (i, 0)),
        out_shape=jax.ShapeDtypeStruct((m, d), F32),
        scratch_shapes=[pltpu.VMEM((tm + HALO, d), BF16), pltpu.VMEM((tm, d), F32)],
        compiler_params=_params("parallel", "arbitrary"),
        name="conv_ffn",
    )(x, x, gain_in.reshape(1, d), wu, wu, conv_w, conv_w, cb, cb, w_down.astype(BF16), gain_out.reshape(1, d))


def _tile_sizes(seq):
    tm = 512 if seq % 512 == 0 else seq
    return dict(tm_proj=min(1024, seq), tm=tm, tf=512, ts=min(512, seq), cpb=min(8, seq // RWKV_CHUNK))


def kernel(x, mem, rel_bias_table, sandwich_gains, mem_src_gain, w_in, w_out, attn_out_gain, sgu_norm_gain, sgu_w, sgu_b, sgu_out_gain, rwkv_mu, rwkv_w0, rwkv_w_up, rwkv_a0, rwkv_a_up, rwkv_g_up, rwkv_k_k, rwkv_k_a, rwkv_r_k, rwkv_ln_gain, rwkv_ln_bias, mem_wq, mem_wkv, mem_wo, ffn_w_up, ffn_conv_w, ffn_conv_b, ffn_w_down):
    b, s, d = x.shape
    depth = w_in.shape[0]
    a_heads = rel_bias_table.shape[1]
    aw = a_heads * HEAD_DIM
    bw = sgu_norm_gain.shape[1]
    cw = rwkv_w0.shape[1]
    c_heads = cw // HEAD_DIM
    t = _tile_sizes(s)
    m = b * s
    xf = x.reshape(m, d)
    memf = mem.reshape(b * mem.shape[1], d)
    for l in range(depth):
        g = sandwich_gains[l]
        wi = w_in[l].astype(BF16)
        za = rms_matmul(xf, g[0], wi[:, :3 * aw], t["tm_proj"], 3 * aw // 3)
        zb = rms_matmul(xf, g[0], wi[:, 3 * aw:3 * aw + 2 * bw], t["tm_proj"], bw)
        zc = rms_matmul(xf, g[0], wi[:, 3 * aw + 2 * bw:], t["tm_proj"], (wi.shape[1] - 3 * aw - 2 * bw) // 3)
        oa = dilated_attention(za.reshape(b, s, 3 * aw), rel_bias_table, a_heads)
        ob = spatial_gating(zb, sgu_norm_gain[l], sgu_w[l], sgu_b[l], sgu_out_gain[l], t["tm"])
        mc, qp, nc, y1, bonus, gate = rwkv_chunks(
            zc.reshape(b, s, -1), rwkv_mu[l], rwkv_w0[l], rwkv_w_up[l], rwkv_a0[l], rwkv_a_up[l], rwkv_g_up[l],
            rwkv_k_k[l], rwkv_k_a[l], rwkv_r_k[l], c_heads, t["ts"])
        oc = rwkv_state_scan(mc, qp, nc, y1, bonus, gate, rwkv_ln_gain[l], rwkv_ln_bias[l], t["cpb"])
        xf = mixer_out(oa.reshape(m, aw), ob, oc.reshape(m, cw), attn_out_gain[l], w_out[l], xf, g[1], t["tm"])
        kv = rms_matmul(memf, mem_src_gain[l], mem_wkv[l].astype(BF16), memf.shape[0], mem_wkv.shape[2] // 2,
                        out_dtype=BF16)
        xf = cross_attention(xf, g[2], mem_wq[l], kv, mem_wo[l], g[3], s, t["tm"])
        xf = conv_ffn(xf, g[4], ffn_w_up[l], ffn_conv_w[l], ffn_conv_b[l], ffn_w_down[l], g[5], s, t["tm"], t["tf"])
    return xf.reshape(b, s, d)
```

```python
import functools
import itertools

import numpy as np
import jax
import jax.numpy as jnp
from jax import lax
from jax.experimental import pallas as pl
from jax.experimental.pallas import tpu as pltpu

HEAD_DIM = 64
LANES = 128
DILATED_PATTERNS = ((128, 1), (512, 4), (2048, 16))
BLOCK = 128
REL_BUCKETS = 32
REL_MAX_DISTANCE = 2048
CHUNK = 128
RWKV_CHUNK = 64
W_LORA = 64
A_LORA = 64
G_LORA = 256
RWKV_GN_EPS = 64e-5
MEM_HEAD_DIM = 128
CONV_WIDTH = 3
NORM_EPS = 1e-6
NEG_INF = -1e30
HALO = 8
FFN_HALO = 16

F32 = jnp.float32
BF16 = jnp.bfloat16
VMEM_LIMIT_BYTES = 56 * 1024 * 1024


def _params(*sem):
    return pltpu.CompilerParams(dimension_semantics=sem, vmem_limit_bytes=VMEM_LIMIT_BYTES)


def _dot(a, b):
    return jnp.dot(a, b, preferred_element_type=F32)


def _dot_nt(a, b):
    return lax.dot_general(a, b, (((1,), (1,)), ((), ())), preferred_element_type=F32)


def _dot_tn(a, b):
    return lax.dot_general(a, b, (((0,), (0,)), ((), ())), preferred_element_type=F32)


def _rms_rows(x, gain):
    ms = jnp.mean(x * x, axis=-1, keepdims=True)
    return x * lax.rsqrt(ms + NORM_EPS) * gain


def _gelu_tanh(x):
    return 0.5 * x * (1.0 + jnp.tanh(np.sqrt(2.0 / np.pi).astype(np.float32) * (x + 0.044715 * (x * x * x))))


def _sigmoid(x):
    return 1.0 / (1.0 + jnp.exp(-x))


def _rms_matmul_kernel(x_ref, g_ref, w_ref, o_ref, h_ref, *, sub):
    @pl.when(pl.program_id(1) == 0)
    def _():
        def body(i, c):
            rows = pl.ds(pl.multiple_of(i * sub, sub), sub)
            h_ref[rows, :] = _rms_rows(x_ref[rows, :], g_ref[...]).astype(BF16)
            return c
        lax.fori_loop(0, x_ref.shape[0] // sub, body, 0)

    o_ref[...] = _dot(h_ref[...], w_ref[...]).astype(o_ref.dtype)


def rms_matmul(x, gain, w, tm, tn, out_dtype=F32):
    m, k = x.shape
    n = w.shape[1]
    assert m % tm == 0 and n % tn == 0
    sub = min(tm, 256)
    return pl.pallas_call(
        functools.partial(_rms_matmul_kernel, sub=sub),
        grid=(m // tm, n // tn),
        in_specs=[pl.BlockSpec((tm, k), lambda i, j: (i, 0)),
                  pl.BlockSpec((1, k), lambda i, j: (0, 0)),
                  pl.BlockSpec((k, tn), lambda i, j: (0, j))],
        out_specs=pl.BlockSpec((tm, tn), lambda i, j: (i, j)),
        out_shape=jax.ShapeDtypeStruct((m, n), out_dtype),
        scratch_shapes=[pltpu.VMEM((tm, k), BF16)],
        compiler_params=_params("parallel", "arbitrary"),
        name="rms_matmul",
    )(x, gain.reshape(1, k), w)


def _t5_causal_bucket(dist):
    max_exact = REL_BUCKETS // 2
    d = np.maximum(dist, 0)
    scaled = np.log(np.maximum(d, 1) / max_exact) / np.log(REL_MAX_DISTANCE / max_exact)
    large = np.minimum(max_exact + (scaled * (REL_BUCKETS - max_exact)).astype(np.int32), REL_BUCKETS - 1)
    return np.where(d < max_exact, d, large).astype(np.int32)


def _bucket_tables():
    out = []
    qi = np.arange(BLOCK)[:, None]
    kj = np.arange(2 * BLOCK)[None, :]
    for window, dilation in DILATED_PATTERNS:
        steps = window // dilation
        assert steps <= BLOCK
        delta = qi + BLOCK - kj
        band = (delta >= 0) & (delta <= steps)
        bucket = _t5_causal_bucket(np.clip(delta, 0, steps) * dilation)
        out.append(np.where(band, bucket, -1))
    return np.stack(out).astype(np.int32)


def _dilated_attn_kernel(table_ref, bucket_ref, q_ref, k_ref, v_ref, o_ref, bias_ref, acc_ref, lse_ref,
                         *, seq, n_pairs, unroll):
    hp = pl.program_id(0)
    lane = lax.broadcasted_iota(jnp.int32, (BLOCK, LANES), 1)
    lo_half = lane < HEAD_DIM

    @pl.when(pl.program_id(1) == 0)
    def _():
        key = lax.broadcasted_iota(jnp.int32, (BLOCK, 2 * BLOCK), 1)
        for p in range(len(DILATED_PATTERNS)):
            bucket = bucket_ref[p]
            for h in range(2):
                val = jnp.zeros(bucket.shape, F32)
                for b in range(REL_BUCKETS):
                    val = jnp.where(bucket == b, table_ref[b, 2 * hp + h], val)
                val = jnp.where(bucket < 0, NEG_INF, val)
                bias_ref[p, h, 0] = val
                bias_ref[p, h, 1] = jnp.where(key < BLOCK, NEG_INF, val)

    scale = HEAD_DIM ** -0.5
    for p, (window, d) in enumerate(DILATED_PATTERNS):
        n_blk = seq // (d * BLOCK)
        shift = int(np.log2(d))

        def block(idx, p=p, d=d, shift=shift):
            r = idx & (d - 1)
            n = idx >> shift
            base = r + n * (BLOCK * d)
            prev = r + jnp.maximum(n - 1, 0) * (BLOCK * d)
            if d == 1:
                cur_rows = pl.ds(pl.multiple_of(base, BLOCK), BLOCK)
                prev_rows = pl.ds(pl.multiple_of(prev, BLOCK), BLOCK)
            else:
                cur_rows = pl.ds(base, BLOCK, stride=d)
                prev_rows = pl.ds(prev, BLOCK, stride=d)
            q = q_ref[cur_rows, :] * scale
            kcat = jnp.concatenate([k_ref[prev_rows, :], k_ref[cur_rows, :]], axis=0).astype(BF16)
            vcat = jnp.concatenate([v_ref[prev_rows, :], v_ref[cur_rows, :]], axis=0).astype(BF16)
            var = jnp.where(n == 0, 1, 0)
            scores = [_dot_nt(jnp.where(lo_half if h == 0 else ~lo_half, q, 0.0).astype(BF16), kcat)
                      for h in range(2)]
            yield
            outs, lses, dens = [], [], []
            for h in range(2):
                s = scores[h] + bias_ref[p, h, var]
                m = jnp.max(s, axis=-1, keepdims=True)
                e = jnp.exp(s - m)
                den = jnp.sum(e, axis=-1, keepdims=True)
                outs.append(_dot(e.astype(BF16), vcat))
                dens.append(den)
                lses.append(m + jnp.log(den))
            yield
            acc_ref[p, cur_rows, :] = jnp.where(lo_half, outs[0] * (1.0 / dens[0]), outs[1] * (1.0 / dens[1]))
            lse_ref[p, cur_rows, :] = jnp.where(lo_half, lses[0], lses[1])

        def block_group(i, carry, block=block):
            stages = [block(i * unroll + u) for u in range(unroll)]
            for _ in itertools.zip_longest(*stages):
                pass
            return carry

        lax.fori_loop(0, d * n_blk // unroll, block_group, 0)

    def merge(i, carry):
        rows = pl.ds(pl.multiple_of(i * BLOCK, BLOCK), BLOCK)
        l0, l1, l2 = lse_ref[0, rows, :], lse_ref[1, rows, :], lse_ref[2, rows, :]
        mx = jnp.maximum(jnp.maximum(l0, l1), l2)
        w0, w1, w2 = jnp.exp(l0 - mx), jnp.exp(l1 - mx), jnp.exp(l2 - mx)
        num = w0 * acc_ref[0, rows, :] + w1 * acc_ref[1, rows, :] + w2 * acc_ref[2, rows, :]
        o_ref[rows, :] = num / (w0 + w1 + w2)
        return carry

    lax.fori_loop(0, seq // BLOCK, merge, 0)


def dilated_attention(za, rel_bias_table, a_heads):
    b, s, _ = za.shape
    n_pairs = a_heads // 2
    unroll = 8
    assert s % (DILATED_PATTERNS[-1][1] * BLOCK) == 0 and (s // BLOCK) % unroll == 0
    buckets = jnp.asarray(_bucket_tables())
    n_pat = len(DILATED_PATTERNS)
    return pl.pallas_call(
        functools.partial(_dilated_attn_kernel, seq=s, n_pairs=n_pairs, unroll=unroll),
        grid=(n_pairs, b),
        in_specs=[pl.BlockSpec(memory_space=pltpu.SMEM),
                  pl.BlockSpec((n_pat, BLOCK, 2 * BLOCK), lambda hp, bi: (0, 0, 0)),
                  pl.BlockSpec((None, s, LANES), lambda hp, bi: (bi, 0, hp)),
                  pl.BlockSpec((None, s, LANES), lambda hp, bi: (bi, 0, n_pairs + hp)),
                  pl.BlockSpec((None, s, LANES), lambda hp, bi: (bi, 0, 2 * n_pairs + hp))],
        out_specs=pl.BlockSpec((None, s, LANES), lambda hp, bi: (bi, 0, hp)),
        out_shape=jax.ShapeDtypeStruct((b, s, a_heads * HEAD_DIM), F32),
        scratch_shapes=[pltpu.VMEM((n_pat, 2, 2, BLOCK, 2 * BLOCK), F32),
                        pltpu.VMEM((n_pat, s, LANES), F32),
                        pltpu.VMEM((n_pat, s, LANES), F32)],
        compiler_params=_params("parallel", "arbitrary"),
        name="dilated_attention",
    )(rel_bias_table, buckets, za, za, za)


def _sgu_kernel(z_ref, ng_ref, w_ref, b_ref, og_ref, o_ref, *, width, groups):
    z = _gelu_tanh(z_ref[...])
    u = z[:, :width]
    g = z[:, width:]
    gc = g - jnp.mean(g, axis=-1, keepdims=True)
    gn = gc * lax.rsqrt(jnp.mean(gc * gc, axis=-1, keepdims=True) + NORM_EPS) * ng_ref[...]
    ri = lax.broadcasted_iota(jnp.int32, (CHUNK, CHUNK), 0)
    ci = lax.broadcasted_iota(jnp.int32, (CHUNK, CHUNK), 1)
    causal = ri >= ci
    lane = lax.broadcasted_iota(jnp.int32, (CHUNK, LANES), 1)
    lo_half = lane < HEAD_DIM
    rows = z.shape[0]
    cols = []
    for j in range(groups // 2):
        w0 = jnp.where(causal, w_ref[2 * j], 0.0).astype(BF16)
        w1 = jnp.where(causal, w_ref[2 * j + 1], 0.0).astype(BF16)
        parts = []
        for c in range(rows // CHUNK):
            blk = gn[c * CHUNK:(c + 1) * CHUNK, j * LANES:(j + 1) * LANES]
            g0 = jnp.where(lo_half, blk, 0.0).astype(BF16)
            g1 = jnp.where(lo_half, 0.0, blk).astype(BF16)
            parts.append(_dot(w0, g0) + _dot(w1, g1) + b_ref[:, j * LANES:(j + 1) * LANES])
        cols.append(jnp.concatenate(parts, axis=0))
    mixed = jnp.concatenate(cols, axis=1)
    o_ref[...] = _rms_rows(u * mixed, og_ref[...]).astype(o_ref.dtype)


def spatial_gating(zb, norm_gain, w_s, b_s, out_gain, tm):
    m, two_w = zb.shape
    width = two_w // 2
    groups = w_s.shape[0]
    bias = jnp.repeat(b_s.T, HEAD_DIM, axis=1)
    return pl.pallas_call(
        functools.partial(_sgu_kernel, width=width, groups=groups),
        grid=(m // tm,),
        in_specs=[pl.BlockSpec((tm, two_w), lambda i: (i, 0)),
                  pl.BlockSpec((1, width), lambda i: (0, 0)),
                  pl.BlockSpec((groups, CHUNK, CHUNK), lambda i: (0, 0, 0)),
                  pl.BlockSpec((CHUNK, width), lambda i: (0, 0)),
                  pl.BlockSpec((1, width), lambda i: (0, 0))],
        out_specs=pl.BlockSpec((tm, width), lambda i: (i, 0)),
        out_shape=jax.ShapeDtypeStruct((m, width), BF16),
        compiler_params=_params("parallel"),
        name="spatial_gating",
    )(zb, norm_gain.reshape(1, width), w_s, bias, out_gain.reshape(1, width))


def _head_sums(z, lo_half):
    s0 = jnp.sum(jnp.where(lo_half, z, 0.0), axis=-1, keepdims=True)
    s1 = jnp.sum(jnp.where(lo_half, 0.0, z), axis=-1, keepdims=True)
    return jnp.where(lo_half, s0, s1)


def _stack_heads(z, lo_half):
    return jnp.concatenate([jnp.where(lo_half, z, 0.0), jnp.where(lo_half, 0.0, z)], axis=0)


def _rwkv_chunk_kernel(r_ref, k_ref, v_ref, lo_ref, g0_ref, g1_ref,
                       rp_ref, kp_ref, vp_ref, lop_ref, g0p_ref, g1p_ref,
                       mu_ref, mulo_ref, mug_ref, w0_ref, wup_ref, a0_ref, aup_ref, gup_ref,
                       kk_ref, ka_ref, rk_ref,
                       mc_ref, qp_ref, nc_ref, y1_ref, bonus_ref, gate_ref,
                       rs_ref, ks_ref, vs_ref, los_ref, gs_ref, *, ts, unroll):
    c = RWKV_CHUNK
    first = pl.program_id(2) == 0
    row = lax.broadcasted_iota(jnp.int32, (ts, LANES), 0)

    def shifted(x_ref, halo_ref, mu):
        x = x_ref[...]
        before = jnp.where(first, 0.0, halo_ref[HALO - 1:HALO, :])
        prev = jnp.where(row == 0, before, pltpu.roll(x, 1, axis=0))
        return x + (prev - x) * mu

    rs_ref[...] = shifted(r_ref, rp_ref, mu_ref[0:1, :])
    ks_ref[...] = shifted(k_ref, kp_ref, mu_ref[1:2, :])
    vs_ref[...] = shifted(v_ref, vp_ref, mu_ref[2:3, :])
    los_ref[...] = shifted(lo_ref, lop_ref, mulo_ref[...])
    gs_ref[:, :LANES] = shifted(g0_ref, g0p_ref, mug_ref[:, :LANES])
    gs_ref[:, LANES:] = shifted(g1_ref, g1p_ref, mug_ref[:, LANES:])

    lane = lax.broadcasted_iota(jnp.int32, (c, LANES), 1)
    lo_half = lane < HEAD_DIM
    trow = lax.broadcasted_iota(jnp.int32, (c, LANES), 0)
    ri = lax.broadcasted_iota(jnp.int32, (2 * c, 2 * c), 0)
    ci = lax.broadcasted_iota(jnp.int32, (2 * c, 2 * c), 1)
    same = (ri // c) == (ci // c)
    strict = same & ((ri % c) > (ci % c))
    incl = same & ((ri % c) >= (ci % c))
    diag = ri == ci

    def chunk(j):
        rows = pl.ds(pl.multiple_of(j * c, c), c)
        r = rs_ref[rows, :]
        kx = ks_ref[rows, :]
        v = vs_ref[rows, :]
        lo = los_ref[rows, :]
        gl = gs_ref[rows, :]

        z = w0_ref[...] + _dot(jnp.tanh(lo).astype(BF16), wup_ref[...])
        w_log = -(jnp.maximum(-z, 0.0) + jnp.log(1.0 + jnp.exp(-jnp.abs(z)))) - 0.5
        e = jnp.exp(w_log)
        a = _sigmoid(a0_ref[...] + _dot(lo.astype(BF16), aup_ref[...]))
        gate_ref[rows, :] = _dot(_sigmoid(gl).astype(BF16), gup_ref[...])
        yield
        kk = kx * kk_ref[...]
        kk = kk / jnp.maximum(jnp.sqrt(_head_sums(kk * kk, lo_half)), 1e-12)
        k = kx * (1.0 + (a - 1.0) * ka_ref[...])
        bonus_ref[rows, :] = _head_sums(r * k * rk_ref[...], lo_half) * v

        cs = e
        for sh in (1, 2, 4, 8, 16, 32):
            cs = cs + jnp.where(trow >= sh, pltpu.roll(cs, sh, axis=0), 0.0)
        cs_end = cs[c - 1:c, :]
        p = jnp.exp(-cs)
        p_before = jnp.exp(e - cs)
        p_inv = jnp.exp(cs)
        p_rest = jnp.exp(cs - cs_end)
        be = kk * a
        at = _stack_heads(-kk * p_before, lo_half)
        rt = _stack_heads(r * p, lo_half)
        bt = _stack_heads(be * p_inv, lo_half).astype(BF16)
        kt = _stack_heads(k * p_inv, lo_half).astype(BF16)
        bh = _stack_heads(be * p_rest, lo_half).astype(BF16)
        kh = _stack_heads(k * p_rest, lo_half).astype(BF16)
        vs = _stack_heads(v, lo_half).astype(BF16)

        cc = _dot_nt(jnp.concatenate([at, rt], axis=0).astype(BF16), jnp.concatenate([bt, kt], axis=0))
        yield
        l_ab = jnp.where(strict, cc[:2 * c, :2 * c], 0.0)
        l_ak = jnp.where(strict, cc[:2 * c, 2 * c:], 0.0)
        m_rb = jnp.where(incl, cc[2 * c:, :2 * c], 0.0)
        m_rk = jnp.where(incl, cc[2 * c:, 2 * c:], 0.0)

        x = jnp.concatenate([at, _dot(l_ak.astype(BF16), vs)], axis=1)
        yield
        lp = l_ab.astype(BF16)
        n = 1
        while n < c:
            x = x + _dot(lp, x.astype(BF16))
            yield
            n *= 2
            if n < c:
                lp = _dot(lp, lp).astype(BF16)
        x16 = x.astype(BF16)

        mw = _dot(m_rb.astype(BF16), x16)
        yield
        qp_ref[j] = (rt + mw[:, :LANES]).astype(BF16)
        y1_ref[j] = mw[:, LANES:] + _dot(m_rk.astype(BF16), vs)
        bw = _dot_tn(bh, x16)
        p_end = jnp.exp(-cs_end)
        decay_end = jnp.where(diag, jnp.broadcast_to(p_end, (2 * c, LANES)), 0.0)
        mc_ref[j] = (decay_end + bw[:, :LANES]).astype(BF16)
        nc_ref[j] = bw[:, LANES:] + _dot_tn(kh, vs)

    def chunk_group(i, carry):
        stages = [chunk(i * unroll + u) for u in range(unroll)]
        for _ in itertools.zip_longest(*stages):
            pass
        return carry

    lax.fori_loop(0, ts // (c * unroll), chunk_group, 0)


def rwkv_chunks(zc, mu, w0, w_up, a0, a_up, g_up, k_k, k_a, r_k, c_heads, ts):
    b, s, _ = zc.shape
    cw = c_heads * HEAD_DIM
    n_pairs = c_heads // 2
    c = RWKV_CHUNK
    n_chunks = s // c
    cpb = ts // c
    assert 2 * c == LANES and W_LORA + A_LORA == LANES and G_LORA == 2 * LANES
    col_lo = 3 * n_pairs
    zero = jnp.zeros((A_LORA, cw), F32)
    wup_ext = jnp.concatenate([w_up, zero], axis=0).astype(BF16)
    aup_ext = jnp.concatenate([zero, a_up], axis=0).astype(BF16)
    mu3 = mu[:3 * cw].reshape(3, cw)
    mulo = mu[3 * cw:3 * cw + LANES].reshape(1, LANES)
    mug = mu[3 * cw + LANES:].reshape(1, G_LORA)

    def tok(col):
        return pl.BlockSpec((None, ts, LANES), lambda bi, hp, t, col=col: (bi, t, col(hp)))

    def halo(col):
        return pl.BlockSpec((None, HALO, LANES),
                            lambda bi, hp, t, col=col: (bi, jnp.maximum(t * (ts // HALO) - 1, 0), col(hp)))

    cols = [lambda hp: hp, lambda hp: n_pairs + hp, lambda hp: 2 * n_pairs + hp,
            lambda hp: col_lo, lambda hp: col_lo + 1, lambda hp: col_lo + 2]

    def per_pair(rows):
        return pl.BlockSpec((rows, LANES), lambda bi, hp, t: (0, hp))

    def const(shape):
        return pl.BlockSpec(shape, lambda bi, hp, t: (0,) * len(shape))

    mat = lambda dt: jax.ShapeDtypeStruct((b, n_pairs, n_chunks, LANES, LANES), dt)
    mat_spec = pl.BlockSpec((None, None, cpb, LANES, LANES), lambda bi, hp, t: (bi, hp, t, 0, 0))
    tok_out = pl.BlockSpec((None, ts, LANES), lambda bi, hp, t: (bi, t, hp))
    return pl.pallas_call(
        functools.partial(_rwkv_chunk_kernel, ts=ts, unroll=min(8, cpb)),
        grid=(b, n_pairs, s // ts),
        in_specs=[tok(cf) for cf in cols] + [halo(cf) for cf in cols] + [
            per_pair(3), const((1, LANES)), const((1, G_LORA)),
            per_pair(1), per_pair(LANES), per_pair(1), per_pair(LANES), per_pair(G_LORA),
            per_pair(1), per_pair(1), per_pair(1)],
        out_specs=[mat_spec, mat_spec, mat_spec, mat_spec, tok_out, tok_out],
        out_shape=[mat(BF16), mat(BF16), mat(F32), mat(F32),
                   jax.ShapeDtypeStruct((b, s, cw), F32), jax.ShapeDtypeStruct((b, s, cw), F32)],
        scratch_shapes=[pltpu.VMEM((ts, LANES), F32)] * 4 + [pltpu.VMEM((ts, G_LORA), F32)],
        compiler_params=_params("parallel", "parallel", "arbitrary"),
        name="rwkv_chunks",
    )(zc, zc, zc, zc, zc, zc, zc, zc, zc, zc, zc, zc,
      mu3, mulo, mug, w0.reshape(1, cw), wup_ext, a0.reshape(1, cw), aup_ext, g_up.astype(BF16),
      k_k.reshape(1, cw), k_a.reshape(1, cw), r_k.reshape(1, cw))


def _rwkv_state_kernel(mc_ref, qp_ref, nc_ref, y1_ref, bonus_ref, gate_ref, lg_ref, lb_ref, o_ref, h_ref,
                       *, n_pairs, cpb):
    c = RWKV_CHUNK

    @pl.when(pl.program_id(1) == 0)
    def _():
        h_ref[...] = jnp.zeros(h_ref.shape, F32)

    lane = lax.broadcasted_iota(jnp.int32, (c, LANES), 1)
    lo_half = lane < HEAD_DIM
    inv_n = 1.0 / HEAD_DIM

    def pair(g, j, rows):
        cols = slice(g * LANES, (g + 1) * LANES)
        h16 = h_ref[g].astype(BF16)
        ys = _dot(qp_ref[g, j], h16) + y1_ref[g, j]
        h_ref[g] = _dot(mc_ref[g, j], h16) + nc_ref[g, j]
        yield
        y = ys[:c, :] + ys[c:, :]
        yc = y - _head_sums(y, lo_half) * inv_n
        yn = yc * lax.rsqrt(_head_sums(yc * yc, lo_half) * inv_n + RWKV_GN_EPS)
        yn = yn * lg_ref[:, cols] + lb_ref[:, cols]
        o_ref[rows, cols] = ((yn + bonus_ref[rows, cols]) * gate_ref[rows, cols]).astype(o_ref.dtype)

    def chunk(j, carry):
        rows = pl.ds(pl.multiple_of(j * c, c), c)
        stages = [pair(g, j, rows) for g in range(n_pairs)]
        for _ in itertools.zip_longest(*stages):
            pass
        return carry

    lax.fori_loop(0, cpb, chunk, 0)


def rwkv_state_scan(mc, qp, nc, y1, bonus, gate, ln_gain, ln_bias, cpb):
    b, n_pairs, n_chunks = mc.shape[:3]
    s = n_chunks * RWKV_CHUNK
    cw = n_pairs * LANES
    ts = cpb * RWKV_CHUNK
    mat_spec = pl.BlockSpec((None, n_pairs, cpb, LANES, LANES), lambda bi, t: (bi, 0, t, 0, 0))
    tok_spec = pl.BlockSpec((None, ts, cw), lambda bi, t: (bi, t, 0))
    vec_spec = pl.BlockSpec((1, cw), lambda bi, t: (0, 0))
    return pl.pallas_call(
        functools.partial(_rwkv_state_kernel, n_pairs=n_pairs, cpb=cpb),
        grid=(b, n_chunks // cpb),
        in_specs=[mat_spec, mat_spec, mat_spec, mat_spec, tok_spec, tok_spec, vec_spec, vec_spec],
        out_specs=tok_spec,
        out_shape=jax.ShapeDtypeStruct((b, s, cw), BF16),
        scratch_shapes=[pltpu.VMEM((n_pairs, LANES, LANES), F32)],
        compiler_params=_params("parallel", "arbitrary"),
        name="rwkv_state_scan",
    )(mc, qp, nc, y1, bonus, gate, ln_gain.reshape(1, cw), ln_bias.reshape(1, cw))


def _mixer_out_kernel(oa_ref, ob_ref, oc_ref, ag_ref, wa_ref, wb_ref, wc_ref, x_ref, g_ref, o_ref):
    oa = _rms_rows(oa_ref[...], ag_ref[...]).astype(BF16)
    y = _dot(oa, wa_ref[...]) + _dot(ob_ref[...], wb_ref[...]) + _dot(oc_ref[...], wc_ref[...])
    o_ref[...] = x_ref[...] + _rms_rows(y, g_ref[...])


def mixer_out(oa, ob, oc, attn_gain, w_out, x, gain, tm):
    m, d = x.shape
    aw, bw, cw = oa.shape[1], ob.shape[1], oc.shape[1]
    w = w_out.astype(BF16)
    row = lambda width: pl.BlockSpec((tm, width), lambda i: (i, 0))
    full = lambda r, cdim: pl.BlockSpec((r, cdim), lambda i: (0, 0))
    return pl.pallas_call(
        _mixer_out_kernel,
        grid=(m // tm,),
        in_specs=[row(aw), row(bw), row(cw), full(1, aw), full(aw, d), full(bw, d), full(cw, d),
                  row(d), full(1, d)],
        out_specs=row(d),
        out_shape=jax.ShapeDtypeStruct((m, d), F32),
        compiler_params=_params("parallel"),
        name="mixer_out",
    )(oa, ob, oc, attn_gain.reshape(1, aw), w[:aw], w[aw:aw + bw], w[aw + bw:], x, gain.reshape(1, d))


def _cross_attn_kernel(x_ref, gi_ref, wq_ref, kv_ref, wo_ref, go_ref, o_ref, *, heads):
    x = x_ref[...]
    q = _dot(_rms_rows(x, gi_ref[...]).astype(BF16), wq_ref[...])
    inner = heads * MEM_HEAD_DIM
    scale = MEM_HEAD_DIM ** -0.5
    outs = []
    for h in range(heads):
        cols = slice(h * MEM_HEAD_DIM, (h + 1) * MEM_HEAD_DIM)
        s = _dot_nt(q[:, cols].astype(BF16), kv_ref[:, cols]) * scale
        m = jnp.max(s, axis=-1, keepdims=True)
        e = jnp.exp(s - m)
        p = e / jnp.sum(e, axis=-1, keepdims=True)
        outs.append(_dot(p.astype(BF16), kv_ref[:, inner + h * MEM_HEAD_DIM:inner + (h + 1) * MEM_HEAD_DIM]))
    o = jnp.concatenate(outs, axis=1).astype(BF16)
    o_ref[...] = x + _rms_rows(_dot(o, wo_ref[...]), go_ref[...])


def cross_attention(x, gain_in, wq, kv, wo, gain_out, seq, tm):
    m, d = x.shape
    inner = wq.shape[1]
    heads = inner // MEM_HEAD_DIM
    mem_tokens = kv.shape[0] // (m // seq)
    blocks_per_seq = seq // tm
    full = lambda r, cdim: pl.BlockSpec((r, cdim), lambda i: (0, 0))
    return pl.pallas_call(
        functools.partial(_cross_attn_kernel, heads=heads),
        grid=(m // tm,),
        in_specs=[pl.BlockSpec((tm, d), lambda i: (i, 0)), full(1, d), full(d, inner),
                  pl.BlockSpec((mem_tokens, 2 * inner), lambda i: (i // blocks_per_seq, 0)),
                  full(inner, d), full(1, d)],
        out_specs=pl.BlockSpec((tm, d), lambda i: (i, 0)),
        out_shape=jax.ShapeDtypeStruct((m, d), F32),
        compiler_params=_params("parallel"),
        name="cross_attention",
    )(x, gain_in.reshape(1, d), wq.astype(BF16), kv, wo.astype(BF16), gain_out.reshape(1, d))


def _conv_ffn_kernel(x_ref, xh_ref, gi_ref, wg_ref, wv_ref, cwg_ref, cwv_ref, cbg_ref, cbv_ref, wd_ref, go_ref,
                     o_ref, h_ref, acc_ref, *, tm, blocks_per_seq, sub):
    i = pl.program_id(0)
    f = pl.program_id(1)

    @pl.when(f == 0)
    def _():
        halo = _rms_rows(xh_ref[...], gi_ref[...])
        h_ref[0:FFN_HALO, :] = jnp.where(i % blocks_per_seq == 0, 0.0, halo).astype(BF16)

        def body(t, c):
            rows = pl.ds(pl.multiple_of(t * sub, sub), sub)
            h_ref[pl.ds(pl.multiple_of(FFN_HALO + t * sub, FFN_HALO), sub), :] = (
                _rms_rows(x_ref[rows, :], gi_ref[...]).astype(BF16))
            return c
        lax.fori_loop(0, tm // sub, body, 0)
        acc_ref[...] = jnp.zeros(acc_ref.shape, F32)

    def conv(w_ref, cw_ref, cb_ref):
        up = _dot(h_ref[...], w_ref[...])
        return (cb_ref[...] + pltpu.roll(up, 2, axis=0) * cw_ref[0:1, :] + pltpu.roll(up, 1, axis=0) * cw_ref[1:2, :]
                + up * cw_ref[2:3, :])

    gate = conv(wg_ref, cwg_ref, cbg_ref)
    val = conv(wv_ref, cwv_ref, cbv_ref)
    act = (_gelu_tanh(gate) * val)[FFN_HALO:, :].astype(BF16)
    acc_ref[...] += _dot(act, wd_ref[...])

    @pl.when(f == pl.num_programs(1) - 1)
    def _():
        o_ref[...] = x_ref[...] + _rms_rows(acc_ref[...], go_ref[...])


def conv_ffn(x, gain_in, w_up, conv_w, conv_b, w_down, gain_out, seq, tm, tf):
    m, d = x.shape
    d_ff = w_down.shape[0]
    nf = d_ff // tf
    assert d_ff % tf == 0 and seq % tm == 0 and conv_w.shape[0] == CONV_WIDTH and CONV_WIDTH - 1 <= FFN_HALO
    wu = w_up.astype(BF16)
    cb = conv_b.reshape(1, 2 * d_ff)
    sub = min(tm, 256)
    return pl.pallas_call(
        functools.partial(_conv_ffn_kernel, tm=tm, blocks_per_seq=seq // tm, sub=sub),
        grid=(m // tm, nf),
        in_specs=[pl.BlockSpec((tm, d), lambda i, f: (i, 0)),
                  pl.BlockSpec((FFN_HALO, d), lambda i, f: (jnp.maximum(i * (tm // FFN_HALO) - 1, 0), 0)),
                  pl.BlockSpec((1, d), lambda i, f: (0, 0)),
                  pl.BlockSpec((d, tf), lambda i, f: (0, f)),
                  pl.BlockSpec((d, tf), lambda i, f: (0, nf + f)),
                  pl.BlockSpec((CONV_WIDTH, tf), lambda i, f: (0, f)),
                  pl.BlockSpec((CONV_WIDTH, tf), lambda i, f: (0, nf + f)),
                  pl.BlockSpec((1, tf), lambda i, f: (0, f)),
                  pl.BlockSpec((1, tf), lambda i, f: (0, nf + f)),
                  pl.BlockSpec((tf, d), lambda i, f: (f, 0)),
                  pl.BlockSpec((1, d), lambda i, f: (0, 0))],
        out_specs=pl.BlockSpec((tm, d), lambda i, f: (i, 0)),
        out_shape=jax.ShapeDtypeStruct((m, d), F32),
        scratch_shapes=[pltpu.VMEM((tm + FFN_HALO, d), BF16), pltpu.VMEM((tm, d), F32)],
        compiler_params=_params("parallel", "arbitrary"),
        name="conv_ffn",
    )(x, x, gain_in.reshape(1, d), wu, wu, conv_w, conv_w, cb, cb, w_down.astype(BF16), gain_out.reshape(1, d))


def _tile_sizes(seq):
    tm = 512 if seq % 512 == 0 else seq
    return dict(tm_proj=min(1024, seq), tm=tm, tf=512, ts=min(512, seq), cpb=min(8, seq // RWKV_CHUNK))


def kernel(x, mem, rel_bias_table, sandwich_gains, mem_src_gain, w_in, w_out, attn_out_gain, sgu_norm_gain, sgu_w, sgu_b, sgu_out_gain, rwkv_mu, rwkv_w0, rwkv_w_up, rwkv_a0, rwkv_a_up, rwkv_g_up, rwkv_k_k, rwkv_k_a, rwkv_r_k, rwkv_ln_gain, rwkv_ln_bias, mem_wq, mem_wkv, mem_wo, ffn_w_up, ffn_conv_w, ffn_conv_b, ffn_w_down):
    b, s, d = x.shape
    depth = w_in.shape[0]
    a_heads = rel_bias_table.shape[1]
    aw = a_heads * HEAD_DIM
    bw = sgu_norm_gain.shape[1]
    cw = rwkv_w0.shape[1]
    c_heads = cw // HEAD_DIM
    t = _tile_sizes(s)
    m = b * s
    xf = x.reshape(m, d)
    memf = mem.reshape(b * mem.shape[1], d)
    for l in range(depth):
        g = sandwich_gains[l]
        wi = w_in[l].astype(BF16)
        za = rms_matmul(xf, g[0], wi[:, :3 * aw], t["tm_proj"], 3 * aw // 3)
        zb = rms_matmul(xf, g[0], wi[:, 3 * aw:3 * aw + 2 * bw], t["tm_proj"], bw)
        zc = rms_matmul(xf, g[0], wi[:, 3 * aw + 2 * bw:], t["tm_proj"], (wi.shape[1] - 3 * aw - 2 * bw) // 3)
        oa = dilated_attention(za.reshape(b, s, 3 * aw), rel_bias_table, a_heads)
        ob = spatial_gating(zb, sgu_norm_gain[l], sgu_w[l], sgu_b[l], sgu_out_gain[l], t["tm"])
        mc, qp, nc, y1, bonus, gate = rwkv_chunks(
            zc.reshape(b, s, -1), rwkv_mu[l], rwkv_w0[l], rwkv_w_up[l], rwkv_a0[l], rwkv_a_up[l], rwkv_g_up[l],
            rwkv_k_k[l], rwkv_k_a[l], rwkv_r_k[l], c_heads, t["ts"])
        oc = rwkv_state_scan(mc, qp, nc, y1, bonus, gate, rwkv_ln_gain[l], rwkv_ln_bias[l], t["cpb"])
        xf = mixer_out(oa.reshape(m, aw), ob, oc.reshape(m, cw), attn_out_gain[l], w_out[l], xf, g[1], t["tm"])
        kv = rms_matmul(memf, mem_src_gain[l], mem_wkv[l].astype(BF16), memf.shape[0], mem_wkv.shape[2] // 2,
                        out_dtype=BF16)
        xf = cross_attention(xf, g[2], mem_wq[l], kv, mem_wo[l], g[3], s, t["tm"])
        xf = conv_ffn(xf, g[4], ffn_w_up[l], ffn_conv_w[l], ffn_conv_b[l], ffn_w_down[l], g[5], s, t["tm"], t["tf"])
    return xf.reshape(b, s, d)
```

```python
import functools
import itertools

import numpy as np
import jax
import jax.numpy as jnp
from jax import lax
from jax.experimental import pallas as pl
from jax.experimental.pallas import tpu as pltpu

HEAD_DIM = 64
LANES = 128
DILATED_PATTERNS = ((128, 1), (512, 4), (2048, 16))
BLOCK = 128
REL_BUCKETS = 32
REL_MAX_DISTANCE = 2048
CHUNK = 128
RWKV_CHUNK = 64
W_LORA = 64
A_LORA = 64
G_LORA = 256
RWKV_GN_EPS = 64e-5
MEM_HEAD_DIM = 128
CONV_WIDTH = 3
NORM_EPS = 1e-6
NEG_INF = -1e30
HALO = 16

F32 = jnp.float32
BF16 = jnp.bfloat16
VMEM_LIMIT_BYTES = 56 * 1024 * 1024


def _params(*sem):
    return pltpu.CompilerParams(dimension_semantics=sem, vmem_limit_bytes=VMEM_LIMIT_BYTES)


def _dot(a, b):
    return jnp.dot(a, b, preferred_element_type=F32)


def _dot_nt(a, b):
    return lax.dot_general(a, b, (((1,), (1,)), ((), ())), preferred_element_type=F32)


def _dot_tn(a, b):
    return lax.dot_general(a, b, (((0,), (0,)), ((), ())), preferred_element_type=F32)


def _run_interleaved(stage_generators):
    for _ in itertools.zip_longest(*stage_generators):
        pass


def _rms_rows(x, gain):
    ms = jnp.mean(x * x, axis=-1, keepdims=True)
    return x * lax.rsqrt(ms + NORM_EPS) * gain


def _norm_rows_with_halo(x_ref, xh_ref, g_ref, h_ref, at_seq_start, sub):
    halo = _rms_rows(xh_ref[...], g_ref[...])
    h_ref[0:HALO, :] = jnp.where(at_seq_start, 0.0, halo).astype(BF16)

    def body(t, c):
        rows = pl.ds(pl.multiple_of(t * sub, sub), sub)
        h_ref[pl.ds(pl.multiple_of(HALO + t * sub, HALO), sub), :] = (
            _rms_rows(x_ref[rows, :], g_ref[...]).astype(BF16))
        return c
    lax.fori_loop(0, x_ref.shape[0] // sub, body, 0)


def _gelu_tanh(x):
    return 0.5 * x * (1.0 + jnp.tanh(np.sqrt(2.0 / np.pi).astype(np.float32) * (x + 0.044715 * (x * x * x))))


def _sigmoid(x):
    return 1.0 / (1.0 + jnp.exp(-x))


def _rms_matmul_kernel(x_ref, g_ref, w_ref, o_ref, h_ref, *, sub):
    @pl.when(pl.program_id(1) == 0)
    def _():
        def body(i, c):
            rows = pl.ds(pl.multiple_of(i * sub, sub), sub)
            h_ref[rows, :] = _rms_rows(x_ref[rows, :], g_ref[...]).astype(BF16)
            return c
        lax.fori_loop(0, x_ref.shape[0] // sub, body, 0)

    o_ref[...] = _dot(h_ref[...], w_ref[...]).astype(o_ref.dtype)


def rms_matmul(x, gain, w, tm, tn, out_dtype=F32):
    m, k = x.shape
    n = w.shape[1]
    assert m % tm == 0 and n % tn == 0
    sub = min(tm, 256)
    return pl.pallas_call(
        functools.partial(_rms_matmul_kernel, sub=sub),
        grid=(m // tm, n // tn),
        in_specs=[pl.BlockSpec((tm, k), lambda i, j: (i, 0)),
                  pl.BlockSpec((1, k), lambda i, j: (0, 0)),
                  pl.BlockSpec((k, tn), lambda i, j: (0, j))],
        out_specs=pl.BlockSpec((tm, tn), lambda i, j: (i, j)),
        out_shape=jax.ShapeDtypeStruct((m, n), out_dtype),
        scratch_shapes=[pltpu.VMEM((tm, k), BF16)],
        compiler_params=_params("parallel", "arbitrary"),
        name="rms_matmul",
    )(x, gain.reshape(1, k), w)


def _in_proj_kernel(x_ref, xh_ref, g_ref, w_ref, mu_ref, o_ref, h_ref, *, blocks_per_seq, sub):
    @pl.when(pl.program_id(1) == 0)
    def _():
        _norm_rows_with_halo(x_ref, xh_ref, g_ref, h_ref, pl.program_id(0) % blocks_per_seq == 0, sub)

    z = _dot(h_ref[...], w_ref[...])
    shifted = z + (pltpu.roll(z, 1, axis=0) - z) * mu_ref[...]
    o_ref[...] = shifted[HALO:, :]


def in_projection(x, gain, w, mu, seq, tm, tn):
    m, k = x.shape
    n = w.shape[1]
    assert m % tm == 0 and n % tn == 0 and seq % tm == 0
    sub = min(tm, 256)
    return pl.pallas_call(
        functools.partial(_in_proj_kernel, blocks_per_seq=seq // tm, sub=sub),
        grid=(m // tm, n // tn),
        in_specs=[pl.BlockSpec((tm, k), lambda i, j: (i, 0)),
                  pl.BlockSpec((HALO, k), lambda i, j: (jnp.maximum(i * (tm // HALO) - 1, 0), 0)),
                  pl.BlockSpec((1, k), lambda i, j: (0, 0)),
                  pl.BlockSpec((k, tn), lambda i, j: (0, j)),
                  pl.BlockSpec((1, tn), lambda i, j: (0, j))],
        out_specs=pl.BlockSpec((tm, tn), lambda i, j: (i, j)),
        out_shape=jax.ShapeDtypeStruct((m, n), F32),
        scratch_shapes=[pltpu.VMEM((tm + HALO, k), BF16)],
        compiler_params=_params("parallel", "arbitrary"),
        name="in_projection",
    )(x, x, gain.reshape(1, k), w, mu.reshape(1, n))


def _t5_causal_bucket(dist):
    max_exact = REL_BUCKETS // 2
    d = np.maximum(dist, 0)
    scaled = np.log(np.maximum(d, 1) / max_exact) / np.log(REL_MAX_DISTANCE / max_exact)
    large = np.minimum(max_exact + (scaled * (REL_BUCKETS - max_exact)).astype(np.int32), REL_BUCKETS - 1)
    return np.where(d < max_exact, d, large).astype(np.int32)


def _bucket_tables():
    out = []
    qi = np.arange(BLOCK)[:, None]
    kj = np.arange(2 * BLOCK)[None, :]
    for window, dilation in DILATED_PATTERNS:
        steps = window // dilation
        assert steps <= BLOCK
        delta = qi + BLOCK - kj
        band = (delta >= 0) & (delta <= steps)
        bucket = _t5_causal_bucket(np.clip(delta, 0, steps) * dilation)
        out.append(np.where(band, bucket, -1))
    return np.stack(out).astype(np.int32)


def _dilated_attn_kernel(table_ref, bucket_ref, q_ref, k_ref, v_ref, o_ref, bias_ref, acc_ref, lse_ref,
                         *, seq, n_pairs, unroll):
    hp = pl.program_id(0)
    lane = lax.broadcasted_iota(jnp.int32, (BLOCK, LANES), 1)
    lo_half = lane < HEAD_DIM

    @pl.when(pl.program_id(1) == 0)
    def _():
        key = lax.broadcasted_iota(jnp.int32, (BLOCK, 2 * BLOCK), 1)
        for p in range(len(DILATED_PATTERNS)):
            bucket = bucket_ref[p]
            for h in range(2):
                val = jnp.zeros(bucket.shape, F32)
                for b in range(REL_BUCKETS):
                    val = jnp.where(bucket == b, table_ref[b, 2 * hp + h], val)
                val = jnp.where(bucket < 0, NEG_INF, val)
                bias_ref[p, h, 0] = val
                bias_ref[p, h, 1] = jnp.where(key < BLOCK, NEG_INF, val)

    scale = HEAD_DIM ** -0.5
    for p, (window, d) in enumerate(DILATED_PATTERNS):
        n_blk = seq // (d * BLOCK)
        shift = int(np.log2(d))

        def block(idx, p=p, d=d, shift=shift):
            r = idx & (d - 1)
            n = idx >> shift
            base = r + n * (BLOCK * d)
            prev = r + jnp.maximum(n - 1, 0) * (BLOCK * d)
            if d == 1:
                cur_rows = pl.ds(pl.multiple_of(base, BLOCK), BLOCK)
                prev_rows = pl.ds(pl.multiple_of(prev, BLOCK), BLOCK)
            else:
                cur_rows = pl.ds(base, BLOCK, stride=d)
                prev_rows = pl.ds(prev, BLOCK, stride=d)
            q = q_ref[cur_rows, :] * scale
            kcat = jnp.concatenate([k_ref[prev_rows, :], k_ref[cur_rows, :]], axis=0).astype(BF16)
            vcat = jnp.concatenate([v_ref[prev_rows, :], v_ref[cur_rows, :]], axis=0).astype(BF16)
            var = jnp.where(n == 0, 1, 0)
            scores = [_dot_nt(jnp.where(lo_half if h == 0 else ~lo_half, q, 0.0).astype(BF16), kcat)
                      for h in range(2)]
            yield
            outs, lses, dens = [], [], []
            for h in range(2):
                s = scores[h] + bias_ref[p, h, var]
                m = jnp.max(s, axis=-1, keepdims=True)
                e = jnp.exp(s - m)
                den = jnp.sum(e, axis=-1, keepdims=True)
                outs.append(_dot(e.astype(BF16), vcat))
                dens.append(den)
                lses.append(m + jnp.log(den))
            yield
            acc_ref[p, cur_rows, :] = jnp.where(lo_half, outs[0] * (1.0 / dens[0]), outs[1] * (1.0 / dens[1]))
            lse_ref[p, cur_rows, :] = jnp.where(lo_half, lses[0], lses[1])

        def block_group(i, carry, block=block):
            _run_interleaved([block(i * unroll + u) for u in range(unroll)])
            return carry

        lax.fori_loop(0, d * n_blk // unroll, block_group, 0)

    def merge(i, carry):
        rows = pl.ds(pl.multiple_of(i * BLOCK, BLOCK), BLOCK)
        l0, l1, l2 = lse_ref[0, rows, :], lse_ref[1, rows, :], lse_ref[2, rows, :]
        mx = jnp.maximum(jnp.maximum(l0, l1), l2)
        w0, w1, w2 = jnp.exp(l0 - mx), jnp.exp(l1 - mx), jnp.exp(l2 - mx)
        num = w0 * acc_ref[0, rows, :] + w1 * acc_ref[1, rows, :] + w2 * acc_ref[2, rows, :]
        o_ref[rows, :] = num / (w0 + w1 + w2)
        return carry

    lax.fori_loop(0, seq // BLOCK, merge, 0)


def dilated_attention(z, col0, rel_bias_table, a_heads):
    b, s, _ = z.shape
    n_pairs = a_heads // 2
    unroll = 8
    assert s % (DILATED_PATTERNS[-1][1] * BLOCK) == 0 and (s // BLOCK) % unroll == 0
    buckets = jnp.asarray(_bucket_tables())
    n_pat = len(DILATED_PATTERNS)
    return pl.pallas_call(
        functools.partial(_dilated_attn_kernel, seq=s, n_pairs=n_pairs, unroll=unroll),
        grid=(n_pairs, b),
        in_specs=[pl.BlockSpec(memory_space=pltpu.SMEM),
                  pl.BlockSpec((n_pat, BLOCK, 2 * BLOCK), lambda hp, bi: (0, 0, 0)),
                  pl.BlockSpec((None, s, LANES), lambda hp, bi: (bi, 0, col0 + hp)),
                  pl.BlockSpec((None, s, LANES), lambda hp, bi: (bi, 0, col0 + n_pairs + hp)),
                  pl.BlockSpec((None, s, LANES), lambda hp, bi: (bi, 0, col0 + 2 * n_pairs + hp))],
        out_specs=pl.BlockSpec((None, s, LANES), lambda hp, bi: (bi, 0, hp)),
        out_shape=jax.ShapeDtypeStruct((b, s, a_heads * HEAD_DIM), F32),
        scratch_shapes=[pltpu.VMEM((n_pat, 2, 2, BLOCK, 2 * BLOCK), F32),
                        pltpu.VMEM((n_pat, s, LANES), F32),
                        pltpu.VMEM((n_pat, s, LANES), F32)],
        compiler_params=_params("parallel", "arbitrary"),
        name="dilated_attention",
    )(rel_bias_table, buckets, z, z, z)


def _sgu_kernel(z_ref, ng_ref, w_ref, b_ref, og_ref, o_ref, *, width, groups):
    z = _gelu_tanh(z_ref[...])
    u = z[:, :width]
    g = z[:, width:]
    gc = g - jnp.mean(g, axis=-1, keepdims=True)
    gn = gc * lax.rsqrt(jnp.mean(gc * gc, axis=-1, keepdims=True) + NORM_EPS) * ng_ref[...]
    ri = lax.broadcasted_iota(jnp.int32, (CHUNK, CHUNK), 0)
    ci = lax.broadcasted_iota(jnp.int32, (CHUNK, CHUNK), 1)
    causal = ri >= ci
    lane = lax.broadcasted_iota(jnp.int32, (CHUNK, LANES), 1)
    lo_half = lane < HEAD_DIM
    rows = z.shape[0]
    cols = []
    for j in range(groups // 2):
        w0 = jnp.where(causal, w_ref[2 * j], 0.0).astype(BF16)
        w1 = jnp.where(causal, w_ref[2 * j + 1], 0.0).astype(BF16)
        parts = []
        for c in range(rows // CHUNK):
            blk = gn[c * CHUNK:(c + 1) * CHUNK, j * LANES:(j + 1) * LANES]
            g0 = jnp.where(lo_half, blk, 0.0).astype(BF16)
            g1 = jnp.where(lo_half, 0.0, blk).astype(BF16)
            parts.append(_dot(w0, g0) + _dot(w1, g1) + b_ref[:, j * LANES:(j + 1) * LANES])
        cols.append(jnp.concatenate(parts, axis=0))
    mixed = jnp.concatenate(cols, axis=1)
    o_ref[...] = _rms_rows(u * mixed, og_ref[...]).astype(o_ref.dtype)


def spatial_gating(z, col_blk, norm_gain, w_s, b_s, out_gain, tm):
    m = z.shape[0]
    width = norm_gain.shape[0]
    two_w = 2 * width
    groups = w_s.shape[0]
    bias = jnp.repeat(b_s.T, HEAD_DIM, axis=1)
    return pl.pallas_call(
        functools.partial(_sgu_kernel, width=width, groups=groups),
        grid=(m // tm,),
        in_specs=[pl.BlockSpec((tm, two_w), lambda i: (i, col_blk)),
                  pl.BlockSpec((1, width), lambda i: (0, 0)),
                  pl.BlockSpec((groups, CHUNK, CHUNK), lambda i: (0, 0, 0)),
                  pl.BlockSpec((CHUNK, width), lambda i: (0, 0)),
                  pl.BlockSpec((1, width), lambda i: (0, 0))],
        out_specs=pl.BlockSpec((tm, width), lambda i: (i, 0)),
        out_shape=jax.ShapeDtypeStruct((m, width), BF16),
        compiler_params=_params("parallel"),
        name="spatial_gating",
    )(z, norm_gain.reshape(1, width), w_s, bias, out_gain.reshape(1, width))


def _head_sums(z, lo_half):
    s0 = jnp.sum(jnp.where(lo_half, z, 0.0), axis=-1, keepdims=True)
    s1 = jnp.sum(jnp.where(lo_half, 0.0, z), axis=-1, keepdims=True)
    return jnp.where(lo_half, s0, s1)


def _stack_heads(z, lo_half):
    return jnp.concatenate([jnp.where(lo_half, z, 0.0), jnp.where(lo_half, 0.0, z)], axis=0)


def _rwkv_chunk_kernel(r_ref, k_ref, v_ref, lo_ref, g0_ref, g1_ref,
                       w0_ref, wup_ref, a0_ref, aup_ref, gup_ref, kk_ref, ka_ref, rk_ref,
                       mc_ref, qp_ref, nc_ref, y1_ref, bonus_ref, gate_ref, *, ts, unroll):
    c = RWKV_CHUNK
    lane = lax.broadcasted_iota(jnp.int32, (c, LANES), 1)
    lo_half = lane < HEAD_DIM
    trow = lax.broadcasted_iota(jnp.int32, (c, LANES), 0)
    ri = lax.broadcasted_iota(jnp.int32, (2 * c, 2 * c), 0)
    ci = lax.broadcasted_iota(jnp.int32, (2 * c, 2 * c), 1)
    same = (ri // c) == (ci // c)
    strict = same & ((ri % c) > (ci % c))
    incl = same & ((ri % c) >= (ci % c))
    diag = ri == ci

    def chunk(j):
        rows = pl.ds(pl.multiple_of(j * c, c), c)
        r = r_ref[rows, :]
        kx = k_ref[rows, :]
        v = v_ref[rows, :]
        lo = lo_ref[rows, :]
        gl = jnp.concatenate([g0_ref[rows, :], g1_ref[rows, :]], axis=1)

        z = w0_ref[...] + _dot(jnp.tanh(lo).astype(BF16), wup_ref[...])
        w_log = -(jnp.maximum(-z, 0.0) + jnp.log(1.0 + jnp.exp(-jnp.abs(z)))) - 0.5
        e = jnp.exp(w_log)
        a = _sigmoid(a0_ref[...] + _dot(lo.astype(BF16), aup_ref[...]))
        gate_ref[rows, :] = _dot(_sigmoid(gl).astype(BF16), gup_ref[...])
        yield
        kk = kx * kk_ref[...]
        kk = kk / jnp.maximum(jnp.sqrt(_head_sums(kk * kk, lo_half)), 1e-12)
        k = kx * (1.0 + (a - 1.0) * ka_ref[...])
        bonus_ref[rows, :] = _head_sums(r * k * rk_ref[...], lo_half) * v

        cs = e
        for sh in (1, 2, 4, 8, 16, 32):
            cs = cs + jnp.where(trow >= sh, pltpu.roll(cs, sh, axis=0), 0.0)
        cs_end = cs[c - 1:c, :]
        p = jnp.exp(-cs)
        p_before = jnp.exp(e - cs)
        p_inv = jnp.exp(cs)
        p_rest = jnp.exp(cs - cs_end)
        be = kk * a
        at = _stack_heads(-kk * p_before, lo_half)
        rt = _stack_heads(r * p, lo_half)
        bt = _stack_heads(be * p_inv, lo_half).astype(BF16)
        kt = _stack_heads(k * p_inv, lo_half).astype(BF16)
        bh = _stack_heads(be * p_rest, lo_half).astype(BF16)
        kh = _stack_heads(k * p_rest, lo_half).astype(BF16)
        vs = _stack_heads(v, lo_half).astype(BF16)

        cc = _dot_nt(jnp.concatenate([at, rt], axis=0).astype(BF16), jnp.concatenate([bt, kt], axis=0))
        yield
        l_ab = jnp.where(strict, cc[:2 * c, :2 * c], 0.0)
        l_ak = jnp.where(strict, cc[:2 * c, 2 * c:], 0.0)
        m_rb = jnp.where(incl, cc[2 * c:, :2 * c], 0.0)
        m_rk = jnp.where(incl, cc[2 * c:, 2 * c:], 0.0)

        x = jnp.concatenate([at, _dot(l_ak.astype(BF16), vs)], axis=1)
        yield
        lp = l_ab.astype(BF16)
        n = 1
        while n < c:
            x = x + _dot(lp, x.astype(BF16))
            yield
            n *= 2
            if n < c:
                lp = _dot(lp, lp).astype(BF16)
        x16 = x.astype(BF16)

        mw = _dot(m_rb.astype(BF16), x16)
        yield
        qp_ref[j] = (rt + mw[:, :LANES]).astype(BF16)
        y1_ref[j] = mw[:, LANES:] + _dot(m_rk.astype(BF16), vs)
        bw = _dot_tn(bh, x16)
        p_end = jnp.exp(-cs_end)
        decay_end = jnp.where(diag, jnp.broadcast_to(p_end, (2 * c, LANES)), 0.0)
        mc_ref[j] = (decay_end + bw[:, :LANES]).astype(BF16)
        nc_ref[j] = bw[:, LANES:] + _dot_tn(kh, vs)

    def chunk_group(i, carry):
        _run_interleaved([chunk(i * unroll + u) for u in range(unroll)])
        return carry

    lax.fori_loop(0, ts // (c * unroll), chunk_group, 0)


def rwkv_chunks(z, col0, w0, w_up, a0, a_up, g_up, k_k, k_a, r_k, c_heads, ts):
    b, s, _ = z.shape
    cw = c_heads * HEAD_DIM
    n_pairs = c_heads // 2
    c = RWKV_CHUNK
    n_chunks = s // c
    cpb = ts // c
    assert 2 * c == LANES and W_LORA + A_LORA == LANES and G_LORA == 2 * LANES
    col_lo = col0 + 3 * n_pairs
    zero = jnp.zeros((A_LORA, cw), F32)
    wup_ext = jnp.concatenate([w_up, zero], axis=0).astype(BF16)
    aup_ext = jnp.concatenate([zero, a_up], axis=0).astype(BF16)

    def tok(col):
        return pl.BlockSpec((None, ts, LANES), lambda bi, hp, t, col=col: (bi, t, col(hp)))

    cols = [lambda hp: col0 + hp, lambda hp: col0 + n_pairs + hp, lambda hp: col0 + 2 * n_pairs + hp,
            lambda hp: col_lo, lambda hp: col_lo + 1, lambda hp: col_lo + 2]

    def per_pair(rows):
        return pl.BlockSpec((rows, LANES), lambda bi, hp, t: (0, hp))

    mat = lambda dt: jax.ShapeDtypeStruct((b, n_pairs, n_chunks, LANES, LANES), dt)
    mat_spec = pl.BlockSpec((None, None, cpb, LANES, LANES), lambda bi, hp, t: (bi, hp, t, 0, 0))
    tok_out = pl.BlockSpec((None, ts, LANES), lambda bi, hp, t: (bi, t, hp))
    return pl.pallas_call(
        functools.partial(_rwkv_chunk_kernel, ts=ts, unroll=min(8, cpb)),
        grid=(b, n_pairs, s // ts),
        in_specs=[tok(cf) for cf in cols] + [
            per_pair(1), per_pair(LANES), per_pair(1), per_pair(LANES), per_pair(G_LORA),
            per_pair(1), per_pair(1), per_pair(1)],
        out_specs=[mat_spec, mat_spec, mat_spec, mat_spec, tok_out, tok_out],
        out_shape=[mat(BF16), mat(BF16), mat(F32), mat(F32),
                   jax.ShapeDtypeStruct((b, s, cw), F32), jax.ShapeDtypeStruct((b, s, cw), F32)],
        compiler_params=_params("parallel", "parallel", "parallel"),
        name="rwkv_chunks",
    )(z, z, z, z, z, z, w0.reshape(1, cw), wup_ext, a0.reshape(1, cw), aup_ext, g_up.astype(BF16),
      k_k.reshape(1, cw), k_a.reshape(1, cw), r_k.reshape(1, cw))


def _rwkv_state_kernel(mc_ref, qp_ref, nc_ref, y1_ref, bonus_ref, gate_ref, lg_ref, lb_ref, o_ref, h_ref,
                       *, n_pairs, cpb):
    c = RWKV_CHUNK

    @pl.when(pl.program_id(1) == 0)
    def _():
        h_ref[...] = jnp.zeros(h_ref.shape, F32)

    lane = lax.broadcasted_iota(jnp.int32, (c, LANES), 1)
    lo_half = lane < HEAD_DIM
    inv_n = 1.0 / HEAD_DIM

    def pair(g, j, rows):
        cols = slice(g * LANES, (g + 1) * LANES)
        h16 = h_ref[g].astype(BF16)
        ys = _dot(qp_ref[g, j], h16) + y1_ref[g, j]
        h_ref[g] = _dot(mc_ref[g, j], h16) + nc_ref[g, j]
        yield
        y = ys[:c, :] + ys[c:, :]
        yc = y - _head_sums(y, lo_half) * inv_n
        yn = yc * lax.rsqrt(_head_sums(yc * yc, lo_half) * inv_n + RWKV_GN_EPS)
        yn = yn * lg_ref[:, cols] + lb_ref[:, cols]
        o_ref[rows, cols] = ((yn + bonus_ref[rows, cols]) * gate_ref[rows, cols]).astype(o_ref.dtype)

    def chunk(j, carry):
        rows = pl.ds(pl.multiple_of(j * c, c), c)
        _run_interleaved([pair(g, j, rows) for g in range(n_pairs)])
        return carry

    lax.fori_loop(0, cpb, chunk, 0)


def rwkv_state_scan(mc, qp, nc, y1, bonus, gate, ln_gain, ln_bias, cpb):
    b, n_pairs, n_chunks = mc.shape[:3]
    s = n_chunks * RWKV_CHUNK
    cw = n_pairs * LANES
    ts = cpb * RWKV_CHUNK
    mat_spec = pl.BlockSpec((None, n_pairs, cpb, LANES, LANES), lambda bi, t: (bi, 0, t, 0, 0))
    tok_spec = pl.BlockSpec((None, ts, cw), lambda bi, t: (bi, t, 0))
    vec_spec = pl.BlockSpec((1, cw), lambda bi, t: (0, 0))
    return pl.pallas_call(
        functools.partial(_rwkv_state_kernel, n_pairs=n_pairs, cpb=cpb),
        grid=(b, n_chunks // cpb),
        in_specs=[mat_spec, mat_spec, mat_spec, mat_spec, tok_spec, tok_spec, vec_spec, vec_spec],
        out_specs=tok_spec,
        out_shape=jax.ShapeDtypeStruct((b, s, cw), BF16),
        scratch_shapes=[pltpu.VMEM((n_pairs, LANES, LANES), F32)],
        compiler_params=_params("parallel", "arbitrary"),
        name="rwkv_state_scan",
    )(mc, qp, nc, y1, bonus, gate, ln_gain.reshape(1, cw), ln_bias.reshape(1, cw))


def _mixer_out_kernel(oa_ref, ob_ref, oc_ref, ag_ref, wa_ref, wb_ref, wc_ref, x_ref, g_ref, o_ref):
    oa = _rms_rows(oa_ref[...], ag_ref[...]).astype(BF16)
    y = _dot(oa, wa_ref[...]) + _dot(ob_ref[...], wb_ref[...]) + _dot(oc_ref[...], wc_ref[...])
    o_ref[...] = x_ref[...] + _rms_rows(y, g_ref[...])


def mixer_out(oa, ob, oc, attn_gain, w_out, x, gain, tm):
    m, d = x.shape
    aw, bw, cw = oa.shape[1], ob.shape[1], oc.shape[1]
    w = w_out.astype(BF16)
    row = lambda width: pl.BlockSpec((tm, width), lambda i: (i, 0))
    full = lambda r, cdim: pl.BlockSpec((r, cdim), lambda i: (0, 0))
    return pl.pallas_call(
        _mixer_out_kernel,
        grid=(m // tm,),
        in_specs=[row(aw), row(bw), row(cw), full(1, aw), full(aw, d), full(bw, d), full(cw, d),
                  row(d), full(1, d)],
        out_specs=row(d),
        out_shape=jax.ShapeDtypeStruct((m, d), F32),
        compiler_params=_params("parallel"),
        name="mixer_out",
    )(oa, ob, oc, attn_gain.reshape(1, aw), w[:aw], w[aw:aw + bw], w[aw + bw:], x, gain.reshape(1, d))


def _cross_attn_kernel(x_ref, gi_ref, wq_ref, kv_ref, wo_ref, go_ref, o_ref, *, heads):
    x = x_ref[...]
    q = _dot(_rms_rows(x, gi_ref[...]).astype(BF16), wq_ref[...])
    inner = heads * MEM_HEAD_DIM
    scale = MEM_HEAD_DIM ** -0.5
    outs = [None] * heads

    def head(h):
        cols = slice(h * MEM_HEAD_DIM, (h + 1) * MEM_HEAD_DIM)
        s = _dot_nt(q[:, cols].astype(BF16), kv_ref[:, cols]) * scale
        yield
        m = jnp.max(s, axis=-1, keepdims=True)
        e = jnp.exp(s - m)
        p = e * (1.0 / jnp.sum(e, axis=-1, keepdims=True))
        outs[h] = _dot(p.astype(BF16), kv_ref[:, inner + h * MEM_HEAD_DIM:inner + (h + 1) * MEM_HEAD_DIM])

    _run_interleaved([head(h) for h in range(heads)])
    o = jnp.concatenate(outs, axis=1).astype(BF16)
    o_ref[...] = x + _rms_rows(_dot(o, wo_ref[...]), go_ref[...])


def cross_attention(x, gain_in, wq, kv, wo, gain_out, seq, tm):
    m, d = x.shape
    inner = wq.shape[1]
    heads = inner // MEM_HEAD_DIM
    mem_tokens = kv.shape[0] // (m // seq)
    blocks_per_seq = seq // tm
    full = lambda r, cdim: pl.BlockSpec((r, cdim), lambda i: (0, 0))
    return pl.pallas_call(
        functools.partial(_cross_attn_kernel, heads=heads),
        grid=(m // tm,),
        in_specs=[pl.BlockSpec((tm, d), lambda i: (i, 0)), full(1, d), full(d, inner),
                  pl.BlockSpec((mem_tokens, 2 * inner), lambda i: (i // blocks_per_seq, 0)),
                  full(inner, d), full(1, d)],
        out_specs=pl.BlockSpec((tm, d), lambda i: (i, 0)),
        out_shape=jax.ShapeDtypeStruct((m, d), F32),
        compiler_params=_params("parallel"),
        name="cross_attention",
    )(x, gain_in.reshape(1, d), wq.astype(BF16), kv, wo.astype(BF16), gain_out.reshape(1, d))


def _conv_ffn_kernel(x_ref, xh_ref, gi_ref, wg_ref, wv_ref, cwg_ref, cwv_ref, cbg_ref, cbv_ref, wd_ref, go_ref,
                     o_ref, h_ref, acc_ref, *, tm, blocks_per_seq, sub):
    i = pl.program_id(0)
    f = pl.program_id(1)

    @pl.when(f == 0)
    def _():
        _norm_rows_with_halo(x_ref, xh_ref, gi_ref, h_ref, i % blocks_per_seq == 0, sub)
        acc_ref[...] = jnp.zeros(acc_ref.shape, F32)

    def conv(w_ref, cw_ref, cb_ref):
        up = _dot(h_ref[...], w_ref[...])
        return (cb_ref[...] + pltpu.roll(up, 2, axis=0) * cw_ref[0:1, :] + pltpu.roll(up, 1, axis=0) * cw_ref[1:2, :]
                + up * cw_ref[2:3, :])

    gate = conv(wg_ref, cwg_ref, cbg_ref)
    val = conv(wv_ref, cwv_ref, cbv_ref)
    act = (_gelu_tanh(gate) * val)[HALO:, :].astype(BF16)
    acc_ref[...] += _dot(act, wd_ref[...])

    @pl.when(f == pl.num_programs(1) - 1)
    def _():
        o_ref[...] = x_ref[...] + _rms_rows(acc_ref[...], go_ref[...])


def conv_ffn(x, gain_in, w_up, conv_w, conv_b, w_down, gain_out, seq, tm, tf):
    m, d = x.shape
    d_ff = w_down.shape[0]
    nf = d_ff // tf
    assert d_ff % tf == 0 and seq % tm == 0 and conv_w.shape[0] == CONV_WIDTH and CONV_WIDTH - 1 <= HALO
    wu = w_up.astype(BF16)
    cb = conv_b.reshape(1, 2 * d_ff)
    sub = min(tm, 256)
    return pl.pallas_call(
        functools.partial(_conv_ffn_kernel, tm=tm, blocks_per_seq=seq // tm, sub=sub),
        grid=(m // tm, nf),
        in_specs=[pl.BlockSpec((tm, d), lambda i, f: (i, 0)),
                  pl.BlockSpec((HALO, d), lambda i, f: (jnp.maximum(i * (tm // HALO) - 1, 0), 0)),
                  pl.BlockSpec((1, d), lambda i, f: (0, 0)),
                  pl.BlockSpec((d, tf), lambda i, f: (0, f)),
                  pl.BlockSpec((d, tf), lambda i, f: (0, nf + f)),
                  pl.BlockSpec((CONV_WIDTH, tf), lambda i, f: (0, f)),
                  pl.BlockSpec((CONV_WIDTH, tf), lambda i, f: (0, nf + f)),
                  pl.BlockSpec((1, tf), lambda i, f: (0, f)),
                  pl.BlockSpec((1, tf), lambda i, f: (0, nf + f)),
                  pl.BlockSpec((tf, d), lambda i, f: (f, 0)),
                  pl.BlockSpec((1, d), lambda i, f: (0, 0))],
        out_specs=pl.BlockSpec((tm, d), lambda i, f: (i, 0)),
        out_shape=jax.ShapeDtypeStruct((m, d), F32),
        scratch_shapes=[pltpu.VMEM((tm + HALO, d), BF16), pltpu.VMEM((tm, d), F32)],
        compiler_params=_params("parallel", "arbitrary"),
        name="conv_ffn",
    )(x, x, gain_in.reshape(1, d), wu, wu, conv_w, conv_w, cb, cb, w_down.astype(BF16), gain_out.reshape(1, d))


def _tile_sizes(seq):
    tm = 512 if seq % 512 == 0 else seq
    return dict(tm_proj=min(1024, seq), tn_proj=768, tm=tm, tf=512, ts=min(512, seq), cpb=min(8, seq // RWKV_CHUNK))


def kernel(x, mem, rel_bias_table, sandwich_gains, mem_src_gain, w_in, w_out, attn_out_gain, sgu_norm_gain, sgu_w, sgu_b, sgu_out_gain, rwkv_mu, rwkv_w0, rwkv_w_up, rwkv_a0, rwkv_a_up, rwkv_g_up, rwkv_k_k, rwkv_k_a, rwkv_r_k, rwkv_ln_gain, rwkv_ln_bias, mem_wq, mem_wkv, mem_wo, ffn_w_up, ffn_conv_w, ffn_conv_b, ffn_w_down):
    b, s, d = x.shape
    depth = w_in.shape[0]
    a_heads = rel_bias_table.shape[1]
    aw = a_heads * HEAD_DIM
    bw = sgu_norm_gain.shape[1]
    cw = rwkv_w0.shape[1]
    c_heads = cw // HEAD_DIM
    t = _tile_sizes(s)
    m = b * s
    xf = x.reshape(m, d)
    memf = mem.reshape(b * mem.shape[1], d)
    for l in range(depth):
        g = sandwich_gains[l]
        wi = w_in[l]
        n_in = wi.shape[1]
        n_pad = -n_in % t["tn_proj"]
        wi = jnp.concatenate([wi[:, 3 * aw:3 * aw + 2 * bw], wi[:, :3 * aw], wi[:, 3 * aw + 2 * bw:],
                              jnp.zeros((d, n_pad), wi.dtype)], axis=1).astype(BF16)
        col_a = 2 * bw // LANES
        col_c = col_a + 3 * aw // LANES
        mu = jnp.concatenate([jnp.zeros((2 * bw + 3 * aw,), F32), rwkv_mu[l], jnp.zeros((n_pad,), F32)])
        z = in_projection(xf, g[0], wi, mu, s, t["tm_proj"], t["tn_proj"])
        z3 = z.reshape(b, s, n_in + n_pad)
        oa = dilated_attention(z3, col_a, rel_bias_table, a_heads)
        ob = spatial_gating(z, 0, sgu_norm_gain[l], sgu_w[l], sgu_b[l], sgu_out_gain[l], t["tm"])
        mc, qp, nc, y1, bonus, gate = rwkv_chunks(
            z3, col_c, rwkv_w0[l], rwkv_w_up[l], rwkv_a0[l], rwkv_a_up[l], rwkv_g_up[l],
            rwkv_k_k[l], rwkv_k_a[l], rwkv_r_k[l], c_heads, t["ts"])
        oc = rwkv_state_scan(mc, qp, nc, y1, bonus, gate, rwkv_ln_gain[l], rwkv_ln_bias[l], t["cpb"])
        xf = mixer_out(oa.reshape(m, aw), ob, oc.reshape(m, cw), attn_out_gain[l], w_out[l], xf, g[1], t["tm"])
        kv = rms_matmul(memf, mem_src_gain[l], mem_wkv[l].astype(BF16), memf.shape[0], mem_wkv.shape[2] // 2,
                        out_dtype=BF16)
        xf = cross_attention(xf, g[2], mem_wq[l], kv, mem_wo[l], g[3], s, t["tm"])
        xf = conv_ffn(xf, g[4], ffn_w_up[l], ffn_conv_w[l], ffn_conv_b[l], ffn_w_down[l], g[5], s, t["tm"], t["tf"])
    return xf.reshape(b, s, d)
```

```python
import functools
import itertools

import numpy as np
import jax
import jax.numpy as jnp
from jax import lax
from jax.experimental import pallas as pl
from jax.experimental.pallas import tpu as pltpu

HEAD_DIM = 64
LANES = 128
DILATED_PATTERNS = ((128, 1), (512, 4), (2048, 16))
BLOCK = 128
DEINTERLEAVE = 4
REL_BUCKETS = 32
REL_MAX_DISTANCE = 2048
CHUNK = 128
RWKV_CHUNK = 64
W_LORA = 64
A_LORA = 64
G_LORA = 256
RWKV_GN_EPS = 64e-5
MEM_HEAD_DIM = 128
CONV_WIDTH = 3
NORM_EPS = 1e-6
NEG_INF = -1e30
HALO = 16

F32 = jnp.float32
BF16 = jnp.bfloat16
VMEM_LIMIT_BYTES = 56 * 1024 * 1024


def _params(*sem):
    return pltpu.CompilerParams(dimension_semantics=sem, vmem_limit_bytes=VMEM_LIMIT_BYTES)


def _dot(a, b):
    return jnp.dot(a, b, preferred_element_type=F32)


def _dot_nt(a, b):
    return lax.dot_general(a, b, (((1,), (1,)), ((), ())), preferred_element_type=F32)


def _dot_tn(a, b):
    return lax.dot_general(a, b, (((0,), (0,)), ((), ())), preferred_element_type=F32)


def _run_interleaved(stage_generators):
    for _ in itertools.zip_longest(*stage_generators):
        pass


def _rms_rows(x, gain):
    ms = jnp.mean(x * x, axis=-1, keepdims=True)
    return x * lax.rsqrt(ms + NORM_EPS) * gain


def _norm_rows_with_halo(x_ref, xh_ref, g_ref, h_ref, at_seq_start, sub):
    halo = _rms_rows(xh_ref[...], g_ref[...])
    h_ref[0:HALO, :] = jnp.where(at_seq_start, 0.0, halo).astype(BF16)

    def body(t, c):
        rows = pl.ds(pl.multiple_of(t * sub, sub), sub)
        h_ref[pl.ds(pl.multiple_of(HALO + t * sub, HALO), sub), :] = (
            _rms_rows(x_ref[rows, :], g_ref[...]).astype(BF16))
        return c
    lax.fori_loop(0, x_ref.shape[0] // sub, body, 0)


def _gelu_tanh(x):
    return 0.5 * x * (1.0 + jnp.tanh(np.sqrt(2.0 / np.pi).astype(np.float32) * (x + 0.044715 * (x * x * x))))


def _sigmoid(x):
    return 1.0 / (1.0 + jnp.exp(-x))


def _rms_matmul_kernel(x_ref, g_ref, w_ref, o_ref, h_ref, *, sub):
    @pl.when(pl.program_id(1) == 0)
    def _():
        def body(i, c):
            rows = pl.ds(pl.multiple_of(i * sub, sub), sub)
            h_ref[rows, :] = _rms_rows(x_ref[rows, :], g_ref[...]).astype(BF16)
            return c
        lax.fori_loop(0, x_ref.shape[0] // sub, body, 0)

    o_ref[...] = _dot(h_ref[...], w_ref[...]).astype(o_ref.dtype)


def rms_matmul(x, gain, w, tm, tn, out_dtype=F32):
    m, k = x.shape
    n = w.shape[1]
    assert m % tm == 0 and n % tn == 0
    sub = min(tm, 256)
    return pl.pallas_call(
        functools.partial(_rms_matmul_kernel, sub=sub),
        grid=(m // tm, n // tn),
        in_specs=[pl.BlockSpec((tm, k), lambda i, j: (i, 0)),
                  pl.BlockSpec((1, k), lambda i, j: (0, 0)),
                  pl.BlockSpec((k, tn), lambda i, j: (0, j))],
        out_specs=pl.BlockSpec((tm, tn), lambda i, j: (i, j)),
        out_shape=jax.ShapeDtypeStruct((m, n), out_dtype),
        scratch_shapes=[pltpu.VMEM((tm, k), BF16)],
        compiler_params=_params("parallel", "arbitrary"),
        name="rms_matmul",
    )(x, gain.reshape(1, k), w)


def _in_proj_kernel(x_ref, xh_ref, g_ref, w_ref, mu_ref, o_ref, h_ref, *, blocks_per_seq, sub):
    @pl.when(pl.program_id(1) == 0)
    def _():
        _norm_rows_with_halo(x_ref, xh_ref, g_ref, h_ref, pl.program_id(0) % blocks_per_seq == 0, sub)

    z = _dot(h_ref[...], w_ref[...])
    shifted = z + (pltpu.roll(z, 1, axis=0) - z) * mu_ref[...]
    o_ref[...] = shifted[HALO:, :]


def in_projection(x, gain, w, mu, seq, tm, tn):
    m, k = x.shape
    n = w.shape[1]
    assert m % tm == 0 and n % tn == 0 and seq % tm == 0
    sub = min(tm, 256)
    return pl.pallas_call(
        functools.partial(_in_proj_kernel, blocks_per_seq=seq // tm, sub=sub),
        grid=(m // tm, n // tn),
        in_specs=[pl.BlockSpec((tm, k), lambda i, j: (i, 0)),
                  pl.BlockSpec((HALO, k), lambda i, j: (jnp.maximum(i * (tm // HALO) - 1, 0), 0)),
                  pl.BlockSpec((1, k), lambda i, j: (0, 0)),
                  pl.BlockSpec((k, tn), lambda i, j: (0, j)),
                  pl.BlockSpec((1, tn), lambda i, j: (0, j))],
        out_specs=pl.BlockSpec((tm, tn), lambda i, j: (i, j)),
        out_shape=jax.ShapeDtypeStruct((m, n), F32),
        scratch_shapes=[pltpu.VMEM((tm + HALO, k), BF16)],
        compiler_params=_params("parallel", "arbitrary"),
        name="in_projection",
    )(x, x, gain.reshape(1, k), w, mu.reshape(1, n))


def _t5_causal_bucket(dist):
    max_exact = REL_BUCKETS // 2
    d = np.maximum(dist, 0)
    scaled = np.log(np.maximum(d, 1) / max_exact) / np.log(REL_MAX_DISTANCE / max_exact)
    large = np.minimum(max_exact + (scaled * (REL_BUCKETS - max_exact)).astype(np.int32), REL_BUCKETS - 1)
    return np.where(d < max_exact, d, large).astype(np.int32)


def _bucket_tables():
    out = []
    qi = np.arange(BLOCK)[:, None]
    kj = np.arange(2 * BLOCK)[None, :]
    for window, dilation in DILATED_PATTERNS:
        steps = window // dilation
        assert steps <= BLOCK
        delta = qi + BLOCK - kj
        band = (delta >= 0) & (delta <= steps)
        bucket = _t5_causal_bucket(np.clip(delta, 0, steps) * dilation)
        out.append(np.where(band, bucket, -1))
    return np.stack(out).astype(np.int32)


def _dilated_attn_kernel(table_ref, bucket_ref, q_ref, k_ref, v_ref, o_ref, bias_ref, acc_ref, lse_ref,
                         q4_ref, k4_ref, v4_ref, *, seq, n_pairs, unroll):
    hp = pl.program_id(0)
    quarter = seq // DEINTERLEAVE
    blocks_per_residue = quarter // BLOCK

    def token_rows(i):
        r4 = i // blocks_per_residue
        blk = i % blocks_per_residue
        return pl.ds(r4 + blk * (DEINTERLEAVE * BLOCK), BLOCK, stride=DEINTERLEAVE)

    def deinterleave(i, carry):
        dst = pl.ds(pl.multiple_of(i * BLOCK, BLOCK), BLOCK)
        src = token_rows(i)
        q4_ref[dst, :] = q_ref[src, :]
        k4_ref[dst, :] = k_ref[src, :]
        v4_ref[dst, :] = v_ref[src, :]
        return carry

    lax.fori_loop(0, seq // BLOCK, deinterleave, 0)
    lane = lax.broadcasted_iota(jnp.int32, (BLOCK, LANES), 1)
    lo_half = lane < HEAD_DIM

    @pl.when(pl.program_id(1) == 0)
    def _():
        key = lax.broadcasted_iota(jnp.int32, (BLOCK, 2 * BLOCK), 1)
        for p in range(len(DILATED_PATTERNS)):
            bucket = bucket_ref[p]
            for h in range(2):
                val = jnp.zeros(bucket.shape, F32)
                for b in range(REL_BUCKETS):
                    val = jnp.where(bucket == b, table_ref[b, 2 * hp + h], val)
                val = jnp.where(bucket < 0, NEG_INF, val)
                bias_ref[p, h, 0] = val
                bias_ref[p, h, 1] = jnp.where(key < BLOCK, NEG_INF, val)

    scale = HEAD_DIM ** -0.5
    for p, (window, d) in enumerate(DILATED_PATTERNS):
        n_blk = seq // (d * BLOCK)
        shift = int(np.log2(d))

        def block(idx, p=p, d=d, shift=shift):
            r = idx & (d - 1)
            n = idx >> shift
            n_prev = jnp.maximum(n - 1, 0)
            if d == 1:
                qs, ks, vs = q_ref, k_ref, v_ref
                cur_rows = pl.ds(pl.multiple_of(n * BLOCK, BLOCK), BLOCK)
                prev_rows = pl.ds(pl.multiple_of(n_prev * BLOCK, BLOCK), BLOCK)
            elif d == DEINTERLEAVE:
                qs, ks, vs = q4_ref, k4_ref, v4_ref
                cur_rows = pl.ds(pl.multiple_of(r * quarter + n * BLOCK, BLOCK), BLOCK)
                prev_rows = pl.ds(pl.multiple_of(r * quarter + n_prev * BLOCK, BLOCK), BLOCK)
            else:
                qs, ks, vs = q4_ref, k4_ref, v4_ref
                sub = d // DEINTERLEAVE
                base = (r % DEINTERLEAVE) * quarter + r // DEINTERLEAVE
                cur_rows = pl.ds(base + n * (sub * BLOCK), BLOCK, stride=sub)
                prev_rows = pl.ds(base + n_prev * (sub * BLOCK), BLOCK, stride=sub)
            q = qs[cur_rows, :] * scale
            kcat = jnp.concatenate([ks[prev_rows, :], ks[cur_rows, :]], axis=0).astype(BF16)
            vcat = jnp.concatenate([vs[prev_rows, :], vs[cur_rows, :]], axis=0).astype(BF16)
            var = jnp.where(n == 0, 1, 0)
            scores = [_dot_nt(jnp.where(lo_half if h == 0 else ~lo_half, q, 0.0).astype(BF16), kcat)
                      for h in range(2)]
            yield
            outs, lses, dens = [], [], []
            for h in range(2):
                s = scores[h] + bias_ref[p, h, var]
                m = jnp.max(s, axis=-1, keepdims=True)
                e = jnp.exp(s - m)
                den = jnp.sum(e, axis=-1, keepdims=True)
                outs.append(_dot(e.astype(BF16), vcat))
                dens.append(den)
                lses.append(m + jnp.log(den))
            yield
            acc_ref[p, cur_rows, :] = jnp.where(lo_half, outs[0] * (1.0 / dens[0]), outs[1] * (1.0 / dens[1]))
            lse_ref[p, cur_rows, :] = jnp.where(lo_half, lses[0], lses[1])

        def block_group(i, carry, block=block):
            _run_interleaved([block(i * unroll + u) for u in range(unroll)])
            return carry

        lax.fori_loop(0, d * n_blk // unroll, block_group, 0)

    def merge(i, carry):
        rows = pl.ds(pl.multiple_of(i * BLOCK, BLOCK), BLOCK)
        tok = token_rows(i)
        l0, l1, l2 = lse_ref[0, tok, :], lse_ref[1, rows, :], lse_ref[2, rows, :]
        mx = jnp.maximum(jnp.maximum(l0, l1), l2)
        w0, w1, w2 = jnp.exp(l0 - mx), jnp.exp(l1 - mx), jnp.exp(l2 - mx)
        num = w0 * acc_ref[0, tok, :] + w1 * acc_ref[1, rows, :] + w2 * acc_ref[2, rows, :]
        o_ref[tok, :] = num / (w0 + w1 + w2)
        return carry

    lax.fori_loop(0, seq // BLOCK, merge, 0)


def dilated_attention(z, col0, rel_bias_table, a_heads):
    b, s, _ = z.shape
    n_pairs = a_heads // 2
    unroll = 8
    assert s % (DILATED_PATTERNS[-1][1] * BLOCK) == 0 and (s // BLOCK) % unroll == 0
    assert [d for _, d in DILATED_PATTERNS] == [1, DEINTERLEAVE, DEINTERLEAVE ** 2]
    buckets = jnp.asarray(_bucket_tables())
    n_pat = len(DILATED_PATTERNS)
    return pl.pallas_call(
        functools.partial(_dilated_attn_kernel, seq=s, n_pairs=n_pairs, unroll=unroll),
        grid=(n_pairs, b),
        in_specs=[pl.BlockSpec(memory_space=pltpu.SMEM),
                  pl.BlockSpec((n_pat, BLOCK, 2 * BLOCK), lambda hp, bi: (0, 0, 0)),
                  pl.BlockSpec((None, s, LANES), lambda hp, bi: (bi, 0, col0 + hp)),
                  pl.BlockSpec((None, s, LANES), lambda hp, bi: (bi, 0, col0 + n_pairs + hp)),
                  pl.BlockSpec((None, s, LANES), lambda hp, bi: (bi, 0, col0 + 2 * n_pairs + hp))],
        out_specs=pl.BlockSpec((None, s, LANES), lambda hp, bi: (bi, 0, hp)),
        out_shape=jax.ShapeDtypeStruct((b, s, a_heads * HEAD_DIM), F32),
        scratch_shapes=[pltpu.VMEM((n_pat, 2, 2, BLOCK, 2 * BLOCK), F32),
                        pltpu.VMEM((n_pat, s, LANES), F32),
                        pltpu.VMEM((n_pat, s, LANES), F32)] + [pltpu.VMEM((s, LANES), F32)] * 3,
        compiler_params=_params("parallel", "arbitrary"),
        name="dilated_attention",
    )(rel_bias_table, buckets, z, z, z)


def _sgu_kernel(z_ref, ng_ref, w_ref, b_ref, og_ref, o_ref, *, width, groups):
    z = _gelu_tanh(z_ref[...])
    u = z[:, :width]
    g = z[:, width:]
    gc = g - jnp.mean(g, axis=-1, keepdims=True)
    gn = gc * lax.rsqrt(jnp.mean(gc * gc, axis=-1, keepdims=True) + NORM_EPS) * ng_ref[...]
    ri = lax.broadcasted_iota(jnp.int32, (CHUNK, CHUNK), 0)
    ci = lax.broadcasted_iota(jnp.int32, (CHUNK, CHUNK), 1)
    causal = ri >= ci
    lane = lax.broadcasted_iota(jnp.int32, (CHUNK, LANES), 1)
    lo_half = lane < HEAD_DIM
    rows = z.shape[0]
    cols = []
    for j in range(groups // 2):
        w0 = jnp.where(causal, w_ref[2 * j], 0.0).astype(BF16)
        w1 = jnp.where(causal, w_ref[2 * j + 1], 0.0).astype(BF16)
        parts = []
        for c in range(rows // CHUNK):
            blk = gn[c * CHUNK:(c + 1) * CHUNK, j * LANES:(j + 1) * LANES]
            g0 = jnp.where(lo_half, blk, 0.0).astype(BF16)
            g1 = jnp.where(lo_half, 0.0, blk).astype(BF16)
            parts.append(_dot(w0, g0) + _dot(w1, g1) + b_ref[:, j * LANES:(j + 1) * LANES])
        cols.append(jnp.concatenate(parts, axis=0))
    mixed = jnp.concatenate(cols, axis=1)
    o_ref[...] = _rms_rows(u * mixed, og_ref[...]).astype(o_ref.dtype)


def spatial_gating(z, col_blk, norm_gain, w_s, b_s, out_gain, tm):
    m = z.shape[0]
    width = norm_gain.shape[0]
    two_w = 2 * width
    groups = w_s.shape[0]
    bias = jnp.repeat(b_s.T, HEAD_DIM, axis=1)
    return pl.pallas_call(
        functools.partial(_sgu_kernel, width=width, groups=groups),
        grid=(m // tm,),
        in_specs=[pl.BlockSpec((tm, two_w), lambda i: (i, col_blk)),
                  pl.BlockSpec((1, width), lambda i: (0, 0)),
                  pl.BlockSpec((groups, CHUNK, CHUNK), lambda i: (0, 0, 0)),
                  pl.BlockSpec((CHUNK, width), lambda i: (0, 0)),
                  pl.BlockSpec((1, width), lambda i: (0, 0))],
        out_specs=pl.BlockSpec((tm, width), lambda i: (i, 0)),
        out_shape=jax.ShapeDtypeStruct((m, width), BF16),
        compiler_params=_params("parallel"),
        name="spatial_gating",
    )(z, norm_gain.reshape(1, width), w_s, bias, out_gain.reshape(1, width))


def _head_sums(z, lo_half):
    s0 = jnp.sum(jnp.where(lo_half, z, 0.0), axis=-1, keepdims=True)
    s1 = jnp.sum(jnp.where(lo_half, 0.0, z), axis=-1, keepdims=True)
    return jnp.where(lo_half, s0, s1)


def _stack_heads(z, lo_half):
    return jnp.concatenate([jnp.where(lo_half, z, 0.0), jnp.where(lo_half, 0.0, z)], axis=0)


def _rwkv_chunk_kernel(r_ref, k_ref, v_ref, lo_ref, g0_ref, g1_ref,
                       w0_ref, wup_ref, a0_ref, aup_ref, gup_ref, kk_ref, ka_ref, rk_ref,
                       mc_ref, qp_ref, nc_ref, y1_ref, bonus_ref, gate_ref, e_ref, a_ref, *, ts, unroll):
    c = RWKV_CHUNK

    lo = lo_ref[...]
    z = w0_ref[...] + _dot(jnp.tanh(lo).astype(BF16), wup_ref[...])
    w_log = -(jnp.maximum(-z, 0.0) + jnp.log(1.0 + jnp.exp(-jnp.abs(z)))) - 0.5
    e_ref[...] = jnp.exp(w_log)
    a_ref[...] = _sigmoid(a0_ref[...] + _dot(lo.astype(BF16), aup_ref[...]))
    gl = jnp.concatenate([g0_ref[...], g1_ref[...]], axis=1)
    gate_ref[...] = _dot(_sigmoid(gl).astype(BF16), gup_ref[...])
    lane = lax.broadcasted_iota(jnp.int32, (c, LANES), 1)
    lo_half = lane < HEAD_DIM
    trow = lax.broadcasted_iota(jnp.int32, (c, LANES), 0)
    ri = lax.broadcasted_iota(jnp.int32, (2 * c, 2 * c), 0)
    ci = lax.broadcasted_iota(jnp.int32, (2 * c, 2 * c), 1)
    same = (ri // c) == (ci // c)
    strict = same & ((ri % c) > (ci % c))
    incl = same & ((ri % c) >= (ci % c))
    diag = ri == ci

    def chunk(j):
        rows = pl.ds(pl.multiple_of(j * c, c), c)
        r = r_ref[rows, :]
        kx = k_ref[rows, :]
        v = v_ref[rows, :]
        e = e_ref[rows, :]
        a = a_ref[rows, :]
        kk = kx * kk_ref[...]
        kk = kk / jnp.maximum(jnp.sqrt(_head_sums(kk * kk, lo_half)), 1e-12)
        k = kx * (1.0 + (a - 1.0) * ka_ref[...])
        bonus_ref[rows, :] = _head_sums(r * k * rk_ref[...], lo_half) * v

        cs = e
        for sh in (1, 2, 4, 8, 16, 32):
            cs = cs + jnp.where(trow >= sh, pltpu.roll(cs, sh, axis=0), 0.0)
        cs_end = cs[c - 1:c, :]
        p = jnp.exp(-cs)
        p_before = jnp.exp(e - cs)
        p_inv = jnp.exp(cs)
        p_rest = jnp.exp(cs - cs_end)
        be = kk * a
        at = _stack_heads(-kk * p_before, lo_half)
        rt = _stack_heads(r * p, lo_half)
        bt = _stack_heads(be * p_inv, lo_half).astype(BF16)
        kt = _stack_heads(k * p_inv, lo_half).astype(BF16)
        bh = _stack_heads(be * p_rest, lo_half).astype(BF16)
        kh = _stack_heads(k * p_rest, lo_half).astype(BF16)
        vs = _stack_heads(v, lo_half).astype(BF16)

        cc = _dot_nt(jnp.concatenate([at, rt], axis=0).astype(BF16), jnp.concatenate([bt, kt], axis=0))
        yield
        l_ab = jnp.where(strict, cc[:2 * c, :2 * c], 0.0)
        l_ak = jnp.where(strict, cc[:2 * c, 2 * c:], 0.0)
        m_rb = jnp.where(incl, cc[2 * c:, :2 * c], 0.0)
        m_rk = jnp.where(incl, cc[2 * c:, 2 * c:], 0.0)

        lv = _dot(jnp.concatenate([l_ak, m_rk], axis=0).astype(BF16), vs)
        x = jnp.concatenate([at, lv[:2 * c, :]], axis=1)
        yield
        lp = l_ab.astype(BF16)
        n = 1
        while n < c:
            x = x + _dot(lp, x.astype(BF16))
            yield
            n *= 2
            if n < c:
                lp = _dot(lp, lp).astype(BF16)
        x16 = x.astype(BF16)

        mw = _dot(m_rb.astype(BF16), x16)
        yield
        qp_ref[j] = (rt + mw[:, :LANES]).astype(BF16)
        y1_ref[j] = mw[:, LANES:] + lv[2 * c:, :]
        bw = _dot_tn(bh, x16)
        p_end = jnp.exp(-cs_end)
        decay_end = jnp.where(diag, jnp.broadcast_to(p_end, (2 * c, LANES)), 0.0)
        mc_ref[j] = (decay_end + bw[:, :LANES]).astype(BF16)
        nc_ref[j] = bw[:, LANES:] + _dot_tn(kh, vs)

    def chunk_group(i, carry):
        _run_interleaved([chunk(i * unroll + u) for u in range(unroll)])
        return carry

    lax.fori_loop(0, ts // (c * unroll), chunk_group, 0)


def rwkv_chunks(z, col0, w0, w_up, a0, a_up, g_up, k_k, k_a, r_k, c_heads, ts):
    b, s, _ = z.shape
    cw = c_heads * HEAD_DIM
    n_pairs = c_heads // 2
    c = RWKV_CHUNK
    n_chunks = s // c
    cpb = ts // c
    assert 2 * c == LANES and W_LORA + A_LORA == LANES and G_LORA == 2 * LANES
    col_lo = col0 + 3 * n_pairs
    zero = jnp.zeros((A_LORA, cw), F32)
    wup_ext = jnp.concatenate([w_up, zero], axis=0).astype(BF16)
    aup_ext = jnp.concatenate([zero, a_up], axis=0).astype(BF16)

    def tok(col):
        return pl.BlockSpec((None, ts, LANES), lambda bi, hp, t, col=col: (bi, t, col(hp)))

    cols = [lambda hp: col0 + hp, lambda hp: col0 + n_pairs + hp, lambda hp: col0 + 2 * n_pairs + hp,
            lambda hp: col_lo, lambda hp: col_lo + 1, lambda hp: col_lo + 2]

    def per_pair(rows):
        return pl.BlockSpec((rows, LANES), lambda bi, hp, t: (0, hp))

    mat = lambda dt: jax.ShapeDtypeStruct((b, n_pairs, n_chunks, LANES, LANES), dt)
    mat_spec = pl.BlockSpec((None, None, cpb, LANES, LANES), lambda bi, hp, t: (bi, hp, t, 0, 0))
    tok_out = pl.BlockSpec((None, ts, LANES), lambda bi, hp, t: (bi, t, hp))
    return pl.pallas_call(
        functools.partial(_rwkv_chunk_kernel, ts=ts, unroll=min(16, cpb)),
        grid=(b, n_pairs, s // ts),
        in_specs=[tok(cf) for cf in cols] + [
            per_pair(1), per_pair(LANES), per_pair(1), per_pair(LANES), per_pair(G_LORA),
            per_pair(1), per_pair(1), per_pair(1)],
        out_specs=[mat_spec, mat_spec, mat_spec, mat_spec, tok_out, tok_out],
        out_shape=[mat(BF16), mat(BF16), mat(F32), mat(F32),
                   jax.ShapeDtypeStruct((b, s, cw), F32), jax.ShapeDtypeStruct((b, s, cw), F32)],
        scratch_shapes=[pltpu.VMEM((ts, LANES), F32)] * 2,
        compiler_params=_params("parallel", "parallel", "parallel"),
        name="rwkv_chunks",
    )(z, z, z, z, z, z, w0.reshape(1, cw), wup_ext, a0.reshape(1, cw), aup_ext, g_up.astype(BF16),
      k_k.reshape(1, cw), k_a.reshape(1, cw), r_k.reshape(1, cw))


def _rwkv_state_kernel(mc_ref, qp_ref, nc_ref, y1_ref, bonus_ref, gate_ref, lg_ref, lb_ref, o_ref, h_ref,
                       *, n_pairs, cpb):
    c = RWKV_CHUNK

    @pl.when(pl.program_id(1) == 0)
    def _():
        h_ref[...] = jnp.zeros(h_ref.shape, F32)

    lane = lax.broadcasted_iota(jnp.int32, (c, LANES), 1)
    lo_half = lane < HEAD_DIM
    inv_n = 1.0 / HEAD_DIM

    def pair(g, j, rows):
        cols = slice(g * LANES, (g + 1) * LANES)
        both = _dot(jnp.concatenate([qp_ref[g, j], mc_ref[g, j]], axis=0), h_ref[g].astype(BF16))
        ys = both[:2 * c, :] + y1_ref[g, j]
        h_ref[g] = both[2 * c:, :] + nc_ref[g, j]
        yield
        y = ys[:c, :] + ys[c:, :]
        yc = y - _head_sums(y, lo_half) * inv_n
        yn = yc * lax.rsqrt(_head_sums(yc * yc, lo_half) * inv_n + RWKV_GN_EPS)
        yn = yn * lg_ref[:, cols] + lb_ref[:, cols]
        o_ref[rows, cols] = ((yn + bonus_ref[rows, cols]) * gate_ref[rows, cols]).astype(o_ref.dtype)

    def chunk(j, carry):
        rows = pl.ds(pl.multiple_of(j * c, c), c)
        _run_interleaved([pair(g, j, rows) for g in range(n_pairs)])
        return carry

    lax.fori_loop(0, cpb, chunk, 0)


def rwkv_state_scan(mc, qp, nc, y1, bonus, gate, ln_gain, ln_bias, cpb):
    b, n_pairs, n_chunks = mc.shape[:3]
    s = n_chunks * RWKV_CHUNK
    cw = n_pairs * LANES
    ts = cpb * RWKV_CHUNK
    mat_spec = pl.BlockSpec((None, n_pairs, cpb, LANES, LANES), lambda bi, t: (bi, 0, t, 0, 0))
    tok_spec = pl.BlockSpec((None, ts, cw), lambda bi, t: (bi, t, 0))
    vec_spec = pl.BlockSpec((1, cw), lambda bi, t: (0, 0))
    return pl.pallas_call(
        functools.partial(_rwkv_state_kernel, n_pairs=n_pairs, cpb=cpb),
        grid=(b, n_chunks // cpb),
        in_specs=[mat_spec, mat_spec, mat_spec, mat_spec, tok_spec, tok_spec, vec_spec, vec_spec],
        out_specs=tok_spec,
        out_shape=jax.ShapeDtypeStruct((b, s, cw), BF16),
        scratch_shapes=[pltpu.VMEM((n_pairs, LANES, LANES), F32)],
        compiler_params=_params("parallel", "arbitrary"),
        name="rwkv_state_scan",
    )(mc, qp, nc, y1, bonus, gate, ln_gain.reshape(1, cw), ln_bias.reshape(1, cw))


def _mixer_out_kernel(oa_ref, ob_ref, oc_ref, ag_ref, wa_ref, wb_ref, wc_ref, x_ref, g_ref, o_ref):
    oa = _rms_rows(oa_ref[...], ag_ref[...]).astype(BF16)
    y = _dot(oa, wa_ref[...]) + _dot(ob_ref[...], wb_ref[...]) + _dot(oc_ref[...], wc_ref[...])
    o_ref[...] = x_ref[...] + _rms_rows(y, g_ref[...])


def mixer_out(oa, ob, oc, attn_gain, w_out, x, gain, tm):
    m, d = x.shape
    aw, bw, cw = oa.shape[1], ob.shape[1], oc.shape[1]
    w = w_out.astype(BF16)
    row = lambda width: pl.BlockSpec((tm, width), lambda i: (i, 0))
    full = lambda r, cdim: pl.BlockSpec((r, cdim), lambda i: (0, 0))
    return pl.pallas_call(
        _mixer_out_kernel,
        grid=(m // tm,),
        in_specs=[row(aw), row(bw), row(cw), full(1, aw), full(aw, d), full(bw, d), full(cw, d),
                  row(d), full(1, d)],
        out_specs=row(d),
        out_shape=jax.ShapeDtypeStruct((m, d), F32),
        compiler_params=_params("parallel"),
        name="mixer_out",
    )(oa, ob, oc, attn_gain.reshape(1, aw), w[:aw], w[aw:aw + bw], w[aw + bw:], x, gain.reshape(1, d))


def _cross_attn_kernel(x_ref, gi_ref, wq_ref, kv_ref, wo_ref, go_ref, o_ref, *, heads):
    x = x_ref[...]
    q = _dot(_rms_rows(x, gi_ref[...]).astype(BF16), wq_ref[...])
    inner = heads * MEM_HEAD_DIM
    scale = MEM_HEAD_DIM ** -0.5
    outs = [None] * heads

    def head(h):
        cols = slice(h * MEM_HEAD_DIM, (h + 1) * MEM_HEAD_DIM)
        s = _dot_nt(q[:, cols].astype(BF16), kv_ref[:, cols]) * scale
        yield
        m = jnp.max(s, axis=-1, keepdims=True)
        e = jnp.exp(s - m)
        p = e * (1.0 / jnp.sum(e, axis=-1, keepdims=True))
        outs[h] = _dot(p.astype(BF16), kv_ref[:, inner + h * MEM_HEAD_DIM:inner + (h + 1) * MEM_HEAD_DIM])

    _run_interleaved([head(h) for h in range(heads)])
    o = jnp.concatenate(outs, axis=1).astype(BF16)
    o_ref[...] = x + _rms_rows(_dot(o, wo_ref[...]), go_ref[...])


def cross_attention(x, gain_in, wq, kv, wo, gain_out, seq, tm):
    m, d = x.shape
    inner = wq.shape[1]
    heads = inner // MEM_HEAD_DIM
    mem_tokens = kv.shape[0] // (m // seq)
    blocks_per_seq = seq // tm
    full = lambda r, cdim: pl.BlockSpec((r, cdim), lambda i: (0, 0))
    return pl.pallas_call(
        functools.partial(_cross_attn_kernel, heads=heads),
        grid=(m // tm,),
        in_specs=[pl.BlockSpec((tm, d), lambda i: (i, 0)), full(1, d), full(d, inner),
                  pl.BlockSpec((mem_tokens, 2 * inner), lambda i: (i // blocks_per_seq, 0)),
                  full(inner, d), full(1, d)],
        out_specs=pl.BlockSpec((tm, d), lambda i: (i, 0)),
        out_shape=jax.ShapeDtypeStruct((m, d), F32),
        compiler_params=_params("parallel"),
        name="cross_attention",
    )(x, gain_in.reshape(1, d), wq.astype(BF16), kv, wo.astype(BF16), gain_out.reshape(1, d))


def _conv_ffn_kernel(x_ref, xh_ref, gi_ref, wg_ref, wv_ref, cwg_ref, cwv_ref, cbg_ref, cbv_ref, wd_ref, go_ref,
                     o_ref, h_ref, acc_ref, *, tm, blocks_per_seq, sub):
    i = pl.program_id(0)
    f = pl.program_id(1)

    @pl.when(f == 0)
    def _():
        _norm_rows_with_halo(x_ref, xh_ref, gi_ref, h_ref, i % blocks_per_seq == 0, sub)
        acc_ref[...] = jnp.zeros(acc_ref.shape, F32)

    def conv(w_ref, cw_ref, cb_ref):
        up = _dot(h_ref[...], w_ref[...])
        return (cb_ref[...] + pltpu.roll(up, 2, axis=0) * cw_ref[0:1, :] + pltpu.roll(up, 1, axis=0) * cw_ref[1:2, :]
                + up * cw_ref[2:3, :])

    gate = conv(wg_ref, cwg_ref, cbg_ref)
    val = conv(wv_ref, cwv_ref, cbv_ref)
    act = (_gelu_tanh(gate) * val)[HALO:, :].astype(BF16)
    acc_ref[...] += _dot(act, wd_ref[...])

    @pl.when(f == pl.num_programs(1) - 1)
    def _():
        o_ref[...] = x_ref[...] + _rms_rows(acc_ref[...], go_ref[...])


def conv_ffn(x, gain_in, w_up, conv_w, conv_b, w_down, gain_out, seq, tm, tf):
    m, d = x.shape
    d_ff = w_down.shape[0]
    nf = d_ff // tf
    assert d_ff % tf == 0 and seq % tm == 0 and conv_w.shape[0] == CONV_WIDTH and CONV_WIDTH - 1 <= HALO
    wu = w_up.astype(BF16)
    cb = conv_b.reshape(1, 2 * d_ff)
    sub = min(tm, 256)
    return pl.pallas_call(
        functools.partial(_conv_ffn_kernel, tm=tm, blocks_per_seq=seq // tm, sub=sub),
        grid=(m // tm, nf),
        in_specs=[pl.BlockSpec((tm, d), lambda i, f: (i, 0)),
                  pl.BlockSpec((HALO, d), lambda i, f: (jnp.maximum(i * (tm // HALO) - 1, 0), 0)),
                  pl.BlockSpec((1, d), lambda i, f: (0, 0)),
                  pl.BlockSpec((d, tf), lambda i, f: (0, f)),
                  pl.BlockSpec((d, tf), lambda i, f: (0, nf + f)),
                  pl.BlockSpec((CONV_WIDTH, tf), lambda i, f: (0, f)),
                  pl.BlockSpec((CONV_WIDTH, tf), lambda i, f: (0, nf + f)),
                  pl.BlockSpec((1, tf), lambda i, f: (0, f)),
                  pl.BlockSpec((1, tf), lambda i, f: (0, nf + f)),
                  pl.BlockSpec((tf, d), lambda i, f: (f, 0)),
                  pl.BlockSpec((1, d), lambda i, f: (0, 0))],
        out_specs=pl.BlockSpec((tm, d), lambda i, f: (i, 0)),
        out_shape=jax.ShapeDtypeStruct((m, d), F32),
        scratch_shapes=[pltpu.VMEM((tm + HALO, d), BF16), pltpu.VMEM((tm, d), F32)],
        compiler_params=_params("parallel", "arbitrary"),
        name="conv_ffn",
    )(x, x, gain_in.reshape(1, d), wu, wu, conv_w, conv_w, cb, cb, w_down.astype(BF16), gain_out.reshape(1, d))


def _tile_sizes(seq):
    tm = 512 if seq % 512 == 0 else seq
    return dict(tm_proj=min(1024, seq), tn_proj=768, tm=tm, tf=512, ts=min(1024, seq), cpb=min(16, seq // RWKV_CHUNK))


def kernel(x, mem, rel_bias_table, sandwich_gains, mem_src_gain, w_in, w_out, attn_out_gain, sgu_norm_gain, sgu_w, sgu_b, sgu_out_gain, rwkv_mu, rwkv_w0, rwkv_w_up, rwkv_a0, rwkv_a_up, rwkv_g_up, rwkv_k_k, rwkv_k_a, rwkv_r_k, rwkv_ln_gain, rwkv_ln_bias, mem_wq, mem_wkv, mem_wo, ffn_w_up, ffn_conv_w, ffn_conv_b, ffn_w_down):
    b, s, d = x.shape
    depth = w_in.shape[0]
    a_heads = rel_bias_table.shape[1]
    aw = a_heads * HEAD_DIM
    bw = sgu_norm_gain.shape[1]
    cw = rwkv_w0.shape[1]
    c_heads = cw // HEAD_DIM
    t = _tile_sizes(s)
    m = b * s
    xf = x.reshape(m, d)
    memf = mem.reshape(b * mem.shape[1], d)
    for l in range(depth):
        g = sandwich_gains[l]
        wi = w_in[l]
        n_in = wi.shape[1]
        n_pad = -n_in % t["tn_proj"]
        wi = jnp.concatenate([wi[:, 3 * aw:3 * aw + 2 * bw], wi[:, :3 * aw], wi[:, 3 * aw + 2 * bw:],
                              jnp.zeros((d, n_pad), wi.dtype)], axis=1).astype(BF16)
        col_a = 2 * bw // LANES
        col_c = col_a + 3 * aw // LANES
        mu = jnp.concatenate([jnp.zeros((2 * bw + 3 * aw,), F32), rwkv_mu[l], jnp.zeros((n_pad,), F32)])
        z = in_projection(xf, g[0], wi, mu, s, t["tm_proj"], t["tn_proj"])
        z3 = z.reshape(b, s, n_in + n_pad)
        oa = dilated_attention(z3, col_a, rel_bias_table, a_heads)
        ob = spatial_gating(z, 0, sgu_norm_gain[l], sgu_w[l], sgu_b[l], sgu_out_gain[l], t["tm"])
        mc, qp, nc, y1, bonus, gate = rwkv_chunks(
            z3, col_c, rwkv_w0[l], rwkv_w_up[l], rwkv_a0[l], rwkv_a_up[l], rwkv_g_up[l],
            rwkv_k_k[l], rwkv_k_a[l], rwkv_r_k[l], c_heads, t["ts"])
        oc = rwkv_state_scan(mc, qp, nc, y1, bonus, gate, rwkv_ln_gain[l], rwkv_ln_bias[l], t["cpb"])
        xf = mixer_out(oa.reshape(m, aw), ob, oc.reshape(m, cw), attn_out_gain[l], w_out[l], xf, g[1], t["tm"])
        kv = rms_matmul(memf, mem_src_gain[l], mem_wkv[l].astype(BF16), memf.shape[0], mem_wkv.shape[2] // 2,
                        out_dtype=BF16)
        xf = cross_attention(xf, g[2], mem_wq[l], kv, mem_wo[l], g[3], s, t["tm"])
        xf = conv_ffn(xf, g[4], ffn_w_up[l], ffn_conv_w[l], ffn_conv_b[l], ffn_w_down[l], g[5], s, t["tm"], t["tf"])
    return xf.reshape(b, s, d)
```

```python
import functools
import itertools

import numpy as np
import jax
import jax.numpy as jnp
from jax import lax
from jax.experimental import pallas as pl
from jax.experimental.pallas import tpu as pltpu

HEAD_DIM = 64
LANES = 128
DILATED_PATTERNS = ((128, 1), (512, 4), (2048, 16))
BLOCK = 128
DEINTERLEAVE = 4
REL_BUCKETS = 32
REL_MAX_DISTANCE = 2048
CHUNK = 128
RWKV_CHUNK = 64
W_LORA = 64
A_LORA = 64
G_LORA = 256
RWKV_GN_EPS = 64e-5
MEM_HEAD_DIM = 128
CONV_WIDTH = 3
NORM_EPS = 1e-6
NEG_INF = -1e30
HALO = 16

F32 = jnp.float32
BF16 = jnp.bfloat16
VMEM_LIMIT_BYTES = 56 * 1024 * 1024


def _params(*sem):
    return pltpu.CompilerParams(dimension_semantics=sem, vmem_limit_bytes=VMEM_LIMIT_BYTES)


def _dot(a, b):
    return jnp.dot(a, b, preferred_element_type=F32)


def _dot_nt(a, b):
    return lax.dot_general(a, b, (((1,), (1,)), ((), ())), preferred_element_type=F32)


def _dot_tn(a, b):
    return lax.dot_general(a, b, (((0,), (0,)), ((), ())), preferred_element_type=F32)


def _run_interleaved(stage_generators):
    for _ in itertools.zip_longest(*stage_generators):
        pass


def _rms_rows(x, gain):
    ms = jnp.mean(x * x, axis=-1, keepdims=True)
    return x * lax.rsqrt(ms + NORM_EPS) * gain


def _norm_rows_with_halo(x_ref, xh_ref, g_ref, h_ref, at_seq_start, sub):
    halo = _rms_rows(xh_ref[...], g_ref[...])
    h_ref[0:HALO, :] = jnp.where(at_seq_start, 0.0, halo).astype(BF16)

    def body(t, c):
        rows = pl.ds(pl.multiple_of(t * sub, sub), sub)
        h_ref[pl.ds(pl.multiple_of(HALO + t * sub, HALO), sub), :] = (
            _rms_rows(x_ref[rows, :], g_ref[...]).astype(BF16))
        return c
    lax.fori_loop(0, x_ref.shape[0] // sub, body, 0)


def _gelu_tanh(x):
    return 0.5 * x * (1.0 + jnp.tanh(np.sqrt(2.0 / np.pi).astype(np.float32) * (x + 0.044715 * (x * x * x))))


def _sigmoid(x):
    return 1.0 / (1.0 + jnp.exp(-x))


def _rms_matmul_kernel(x_ref, g_ref, w_ref, o_ref, h_ref, *, sub):
    @pl.when(pl.program_id(1) == 0)
    def _():
        def body(i, c):
            rows = pl.ds(pl.multiple_of(i * sub, sub), sub)
            h_ref[rows, :] = _rms_rows(x_ref[rows, :], g_ref[...]).astype(BF16)
            return c
        lax.fori_loop(0, x_ref.shape[0] // sub, body, 0)

    o_ref[...] = _dot(h_ref[...], w_ref[...]).astype(o_ref.dtype)


def rms_matmul(x, gain, w, tm, tn, out_dtype=F32):
    m, k = x.shape
    n = w.shape[1]
    assert m % tm == 0 and n % tn == 0
    sub = min(tm, 256)
    return pl.pallas_call(
        functools.partial(_rms_matmul_kernel, sub=sub),
        grid=(m // tm, n // tn),
        in_specs=[pl.BlockSpec((tm, k), lambda i, j: (i, 0)),
                  pl.BlockSpec((1, k), lambda i, j: (0, 0)),
                  pl.BlockSpec((k, tn), lambda i, j: (0, j))],
        out_specs=pl.BlockSpec((tm, tn), lambda i, j: (i, j)),
        out_shape=jax.ShapeDtypeStruct((m, n), out_dtype),
        scratch_shapes=[pltpu.VMEM((tm, k), BF16)],
        compiler_params=_params("parallel", "arbitrary"),
        name="rms_matmul",
    )(x, gain.reshape(1, k), w)


def _in_proj_kernel(x_ref, xh_ref, g_ref, w_ref, mu_ref, o_ref, h_ref, *, blocks_per_seq, sub):
    @pl.when(pl.program_id(1) == 0)
    def _():
        _norm_rows_with_halo(x_ref, xh_ref, g_ref, h_ref, pl.program_id(0) % blocks_per_seq == 0, sub)

    z = _dot(h_ref[...], w_ref[...])
    shifted = z + (pltpu.roll(z, 1, axis=0) - z) * mu_ref[...]
    o_ref[...] = shifted[HALO:, :]


def in_projection(x, gain, w, mu, seq, tm, tn):
    m, k = x.shape
    n = w.shape[1]
    assert m % tm == 0 and n % tn == 0 and seq % tm == 0
    sub = min(tm, 256)
    return pl.pallas_call(
        functools.partial(_in_proj_kernel, blocks_per_seq=seq // tm, sub=sub),
        grid=(m // tm, n // tn),
        in_specs=[pl.BlockSpec((tm, k), lambda i, j: (i, 0)),
                  pl.BlockSpec((HALO, k), lambda i, j: (jnp.maximum(i * (tm // HALO) - 1, 0), 0)),
                  pl.BlockSpec((1, k), lambda i, j: (0, 0)),
                  pl.BlockSpec((k, tn), lambda i, j: (0, j)),
                  pl.BlockSpec((1, tn), lambda i, j: (0, j))],
        out_specs=pl.BlockSpec((tm, tn), lambda i, j: (i, j)),
        out_shape=jax.ShapeDtypeStruct((m, n), F32),
        scratch_shapes=[pltpu.VMEM((tm + HALO, k), BF16)],
        compiler_params=_params("parallel", "arbitrary"),
        name="in_projection",
    )(x, x, gain.reshape(1, k), w, mu.reshape(1, n))


def _t5_causal_bucket(dist):
    max_exact = REL_BUCKETS // 2
    d = np.maximum(dist, 0)
    scaled = np.log(np.maximum(d, 1) / max_exact) / np.log(REL_MAX_DISTANCE / max_exact)
    large = np.minimum(max_exact + (scaled * (REL_BUCKETS - max_exact)).astype(np.int32), REL_BUCKETS - 1)
    return np.where(d < max_exact, d, large).astype(np.int32)


def _bucket_tables():
    out = []
    qi = np.arange(BLOCK)[:, None]
    kj = np.arange(2 * BLOCK)[None, :]
    for window, dilation in DILATED_PATTERNS:
        steps = window // dilation
        assert steps <= BLOCK
        delta = qi + BLOCK - kj
        band = (delta >= 0) & (delta <= steps)
        bucket = _t5_causal_bucket(np.clip(delta, 0, steps) * dilation)
        out.append(np.where(band, bucket, -1))
    return np.stack(out).astype(np.int32)


def _dilated_attn_kernel(table_ref, bucket_ref, q_ref, k_ref, v_ref, o_ref, bias_ref, acc_ref, lse_ref,
                         q4_ref, k4_ref, v4_ref, *, seq, n_pairs, unroll):
    hp = pl.program_id(0)
    quarter = seq // DEINTERLEAVE
    blocks_per_residue = quarter // BLOCK

    def token_rows(i):
        r4 = i // blocks_per_residue
        blk = i % blocks_per_residue
        return pl.ds(r4 + blk * (DEINTERLEAVE * BLOCK), BLOCK, stride=DEINTERLEAVE)

    def deinterleave(i, carry):
        dst = pl.ds(pl.multiple_of(i * BLOCK, BLOCK), BLOCK)
        src = token_rows(i)
        q4_ref[dst, :] = q_ref[src, :]
        k4_ref[dst, :] = k_ref[src, :]
        v4_ref[dst, :] = v_ref[src, :]
        return carry

    lax.fori_loop(0, seq // BLOCK, deinterleave, 0)
    lane = lax.broadcasted_iota(jnp.int32, (BLOCK, LANES), 1)
    lo_half = lane < HEAD_DIM

    @pl.when(pl.program_id(1) == 0)
    def _():
        key = lax.broadcasted_iota(jnp.int32, (BLOCK, 2 * BLOCK), 1)
        for p in range(len(DILATED_PATTERNS)):
            bucket = bucket_ref[p]
            for h in range(2):
                val = jnp.zeros(bucket.shape, F32)
                for b in range(REL_BUCKETS):
                    val = jnp.where(bucket == b, table_ref[b, 2 * hp + h], val)
                val = jnp.where(bucket < 0, NEG_INF, val)
                bias_ref[p, h, 0] = val
                bias_ref[p, h, 1] = jnp.where(key < BLOCK, NEG_INF, val)

    scale = HEAD_DIM ** -0.5
    for p, (window, d) in enumerate(DILATED_PATTERNS):
        n_blk = seq // (d * BLOCK)
        shift = int(np.log2(d))

        def block(idx, p=p, d=d, shift=shift):
            r = idx & (d - 1)
            n = idx >> shift
            n_prev = jnp.maximum(n - 1, 0)
            if d == 1:
                qs, ks, vs = q_ref, k_ref, v_ref
                cur_rows = pl.ds(pl.multiple_of(n * BLOCK, BLOCK), BLOCK)
                prev_rows = pl.ds(pl.multiple_of(n_prev * BLOCK, BLOCK), BLOCK)
            elif d == DEINTERLEAVE:
                qs, ks, vs = q4_ref, k4_ref, v4_ref
                cur_rows = pl.ds(pl.multiple_of(r * quarter + n * BLOCK, BLOCK), BLOCK)
                prev_rows = pl.ds(pl.multiple_of(r * quarter + n_prev * BLOCK, BLOCK), BLOCK)
            else:
                qs, ks, vs = q4_ref, k4_ref, v4_ref
                sub = d // DEINTERLEAVE
                base = (r % DEINTERLEAVE) * quarter + r // DEINTERLEAVE
                cur_rows = pl.ds(base + n * (sub * BLOCK), BLOCK, stride=sub)
                prev_rows = pl.ds(base + n_prev * (sub * BLOCK), BLOCK, stride=sub)
            q = qs[cur_rows, :] * scale
            kcat = jnp.concatenate([ks[prev_rows, :], ks[cur_rows, :]], axis=0).astype(BF16)
            vcat = jnp.concatenate([vs[prev_rows, :], vs[cur_rows, :]], axis=0).astype(BF16)
            var = jnp.where(n == 0, 1, 0)
            scores = [_dot_nt(jnp.where(lo_half if h == 0 else ~lo_half, q, 0.0).astype(BF16), kcat)
                      for h in range(2)]
            yield
            outs, lses, dens = [], [], []
            for h in range(2):
                s = scores[h] + bias_ref[p, h, var]
                m = jnp.max(s, axis=-1, keepdims=True)
                e = jnp.exp(s - m)
                den = jnp.sum(e, axis=-1, keepdims=True)
                outs.append(_dot(e.astype(BF16), vcat))
                dens.append(den)
                lses.append(m + jnp.log(den))
            yield
            acc_ref[p, cur_rows, :] = jnp.where(lo_half, outs[0] * (1.0 / dens[0]), outs[1] * (1.0 / dens[1]))
            lse_ref[p, cur_rows, :] = jnp.where(lo_half, lses[0], lses[1])

        def block_group(i, carry, block=block):
            _run_interleaved([block(i * unroll + u) for u in range(unroll)])
            return carry

        lax.fori_loop(0, d * n_blk // unroll, block_group, 0)

    def merge(i, carry):
        rows = pl.ds(pl.multiple_of(i * BLOCK, BLOCK), BLOCK)
        tok = token_rows(i)
        l0, l1, l2 = lse_ref[0, tok, :], lse_ref[1, rows, :], lse_ref[2, rows, :]
        mx = jnp.maximum(jnp.maximum(l0, l1), l2)
        w0, w1, w2 = jnp.exp(l0 - mx), jnp.exp(l1 - mx), jnp.exp(l2 - mx)
        num = w0 * acc_ref[0, tok, :] + w1 * acc_ref[1, rows, :] + w2 * acc_ref[2, rows, :]
        o_ref[tok, :] = num / (w0 + w1 + w2)
        return carry

    lax.fori_loop(0, seq // BLOCK, merge, 0)


def dilated_attention(z, col0, rel_bias_table, a_heads):
    b, s, _ = z.shape
    n_pairs = a_heads // 2
    unroll = 16
    assert s % (DILATED_PATTERNS[-1][1] * BLOCK) == 0 and (s // BLOCK) % unroll == 0
    assert [d for _, d in DILATED_PATTERNS] == [1, DEINTERLEAVE, DEINTERLEAVE ** 2]
    buckets = jnp.asarray(_bucket_tables())
    n_pat = len(DILATED_PATTERNS)
    return pl.pallas_call(
        functools.partial(_dilated_attn_kernel, seq=s, n_pairs=n_pairs, unroll=unroll),
        grid=(n_pairs, b),
        in_specs=[pl.BlockSpec(memory_space=pltpu.SMEM),
                  pl.BlockSpec((n_pat, BLOCK, 2 * BLOCK), lambda hp, bi: (0, 0, 0)),
                  pl.BlockSpec((None, s, LANES), lambda hp, bi: (bi, 0, col0 + hp)),
                  pl.BlockSpec((None, s, LANES), lambda hp, bi: (bi, 0, col0 + n_pairs + hp)),
                  pl.BlockSpec((None, s, LANES), lambda hp, bi: (bi, 0, col0 + 2 * n_pairs + hp))],
        out_specs=pl.BlockSpec((None, s, LANES), lambda hp, bi: (bi, 0, hp)),
        out_shape=jax.ShapeDtypeStruct((b, s, a_heads * HEAD_DIM), F32),
        scratch_shapes=[pltpu.VMEM((n_pat, 2, 2, BLOCK, 2 * BLOCK), F32),
                        pltpu.VMEM((n_pat, s, LANES), F32),
                        pltpu.VMEM((n_pat, s, LANES), F32)] + [pltpu.VMEM((s, LANES), F32)] * 3,
        compiler_params=_params("parallel", "arbitrary"),
        name="dilated_attention",
    )(rel_bias_table, buckets, z, z, z)


def _sgu_kernel(z_ref, ng_ref, w_ref, b_ref, og_ref, o_ref, *, width, groups):
    z = _gelu_tanh(z_ref[...])
    u = z[:, :width]
    g = z[:, width:]
    gc = g - jnp.mean(g, axis=-1, keepdims=True)
    gn = gc * lax.rsqrt(jnp.mean(gc * gc, axis=-1, keepdims=True) + NORM_EPS) * ng_ref[...]
    ri = lax.broadcasted_iota(jnp.int32, (CHUNK, CHUNK), 0)
    ci = lax.broadcasted_iota(jnp.int32, (CHUNK, CHUNK), 1)
    causal = ri >= ci
    lane = lax.broadcasted_iota(jnp.int32, (CHUNK, LANES), 1)
    lo_half = lane < HEAD_DIM
    rows = z.shape[0]
    cols = []
    for j in range(groups // 2):
        w0 = jnp.where(causal, w_ref[2 * j], 0.0).astype(BF16)
        w1 = jnp.where(causal, w_ref[2 * j + 1], 0.0).astype(BF16)
        parts = []
        for c in range(rows // CHUNK):
            blk = gn[c * CHUNK:(c + 1) * CHUNK, j * LANES:(j + 1) * LANES]
            g0 = jnp.where(lo_half, blk, 0.0).astype(BF16)
            g1 = jnp.where(lo_half, 0.0, blk).astype(BF16)
            parts.append(_dot(w0, g0) + _dot(w1, g1) + b_ref[:, j * LANES:(j + 1) * LANES])
        cols.append(jnp.concatenate(parts, axis=0))
    mixed = jnp.concatenate(cols, axis=1)
    o_ref[...] = _rms_rows(u * mixed, og_ref[...]).astype(o_ref.dtype)


def spatial_gating(z, col_blk, norm_gain, w_s, b_s, out_gain, tm):
    m = z.shape[0]
    width = norm_gain.shape[0]
    two_w = 2 * width
    groups = w_s.shape[0]
    bias = jnp.repeat(b_s.T, HEAD_DIM, axis=1)
    return pl.pallas_call(
        functools.partial(_sgu_kernel, width=width, groups=groups),
        grid=(m // tm,),
        in_specs=[pl.BlockSpec((tm, two_w), lambda i: (i, col_blk)),
                  pl.BlockSpec((1, width), lambda i: (0, 0)),
                  pl.BlockSpec((groups, CHUNK, CHUNK), lambda i: (0, 0, 0)),
                  pl.BlockSpec((CHUNK, width), lambda i: (0, 0)),
                  pl.BlockSpec((1, width), lambda i: (0, 0))],
        out_specs=pl.BlockSpec((tm, width), lambda i: (i, 0)),
        out_shape=jax.ShapeDtypeStruct((m, width), BF16),
        compiler_params=_params("parallel"),
        name="spatial_gating",
    )(z, norm_gain.reshape(1, width), w_s, bias, out_gain.reshape(1, width))


def _head_sums(z, lo_half):
    s0 = jnp.sum(jnp.where(lo_half, z, 0.0), axis=-1, keepdims=True)
    s1 = jnp.sum(jnp.where(lo_half, 0.0, z), axis=-1, keepdims=True)
    return jnp.where(lo_half, s0, s1)


def _stack_heads(z, lo_half):
    return jnp.concatenate([jnp.where(lo_half, z, 0.0), jnp.where(lo_half, 0.0, z)], axis=0)


def _rwkv_chunk_kernel(r_ref, k_ref, v_ref, lo_ref, g0_ref, g1_ref,
                       w0_ref, wup_ref, a0_ref, aup_ref, gup_ref, kk_ref, ka_ref, rk_ref,
                       mc_ref, qp_ref, nc_ref, y1_ref, bonus_ref, gate_ref, e_ref, a_ref, *, ts, unroll):
    c = RWKV_CHUNK

    lo = lo_ref[...]
    z = w0_ref[...] + _dot(jnp.tanh(lo).astype(BF16), wup_ref[...])
    w_log = -(jnp.maximum(-z, 0.0) + jnp.log(1.0 + jnp.exp(-jnp.abs(z)))) - 0.5
    e_ref[...] = jnp.exp(w_log)
    a_ref[...] = _sigmoid(a0_ref[...] + _dot(lo.astype(BF16), aup_ref[...]))
    gl = jnp.concatenate([g0_ref[...], g1_ref[...]], axis=1)
    gate_ref[...] = _dot(_sigmoid(gl).astype(BF16), gup_ref[...])
    lane = lax.broadcasted_iota(jnp.int32, (c, LANES), 1)
    lo_half = lane < HEAD_DIM
    trow = lax.broadcasted_iota(jnp.int32, (c, LANES), 0)
    ri = lax.broadcasted_iota(jnp.int32, (2 * c, 2 * c), 0)
    ci = lax.broadcasted_iota(jnp.int32, (2 * c, 2 * c), 1)
    same = (ri // c) == (ci // c)
    strict = same & ((ri % c) > (ci % c))
    incl = same & ((ri % c) >= (ci % c))
    diag = ri == ci

    def chunk(j):
        rows = pl.ds(pl.multiple_of(j * c, c), c)
        r = r_ref[rows, :]
        kx = k_ref[rows, :]
        v = v_ref[rows, :]
        e = e_ref[rows, :]
        a = a_ref[rows, :]
        kk = kx * kk_ref[...]
        kk = kk / jnp.maximum(jnp.sqrt(_head_sums(kk * kk, lo_half)), 1e-12)
        k = kx * (1.0 + (a - 1.0) * ka_ref[...])
        bonus_ref[rows, :] = _head_sums(r * k * rk_ref[...], lo_half) * v

        cs = e
        for sh in (1, 2, 4, 8, 16, 32):
            cs = cs + jnp.where(trow >= sh, pltpu.roll(cs, sh, axis=0), 0.0)
        cs_end = cs[c - 1:c, :]
        p = jnp.exp(-cs)
        p_before = jnp.exp(e - cs)
        p_inv = jnp.exp(cs)
        p_rest = jnp.exp(cs - cs_end)
        be = kk * a
        at = _stack_heads(-kk * p_before, lo_half)
        rt = _stack_heads(r * p, lo_half)
        bt = _stack_heads(be * p_inv, lo_half).astype(BF16)
        kt = _stack_heads(k * p_inv, lo_half).astype(BF16)
        bh = _stack_heads(be * p_rest, lo_half).astype(BF16)
        kh = _stack_heads(k * p_rest, lo_half).astype(BF16)
        vs = _stack_heads(v, lo_half).astype(BF16)

        cc = _dot_nt(jnp.concatenate([at, rt], axis=0).astype(BF16), jnp.concatenate([bt, kt], axis=0))
        yield
        l_ab = jnp.where(strict, cc[:2 * c, :2 * c], 0.0)
        l_ak = jnp.where(strict, cc[:2 * c, 2 * c:], 0.0)
        m_rb = jnp.where(incl, cc[2 * c:, :2 * c], 0.0)
        m_rk = jnp.where(incl, cc[2 * c:, 2 * c:], 0.0)

        lv = _dot(jnp.concatenate([l_ak, m_rk], axis=0).astype(BF16), vs)
        x = jnp.concatenate([at, lv[:2 * c, :]], axis=1)
        yield
        lp = l_ab.astype(BF16)
        n = 1
        while n < c:
            x = x + _dot(lp, x.astype(BF16))
            yield
            n *= 2
            if n < c:
                lp = _dot(lp, lp).astype(BF16)
        x16 = x.astype(BF16)

        mw = _dot(m_rb.astype(BF16), x16)
        yield
        qp_ref[j] = (rt + mw[:, :LANES]).astype(BF16)
        y1_ref[j] = mw[:, LANES:] + lv[2 * c:, :]
        bw = _dot_tn(bh, x16)
        p_end = jnp.exp(-cs_end)
        decay_end = jnp.where(diag, jnp.broadcast_to(p_end, (2 * c, LANES)), 0.0)
        mc_ref[j] = (decay_end + bw[:, :LANES]).astype(BF16)
        nc_ref[j] = bw[:, LANES:] + _dot_tn(kh, vs)

    def chunk_group(i, carry):
        _run_interleaved([chunk(i * unroll + u) for u in range(unroll)])
        return carry

    lax.fori_loop(0, ts // (c * unroll), chunk_group, 0)


def rwkv_chunks(z, col0, w0, w_up, a0, a_up, g_up, k_k, k_a, r_k, c_heads, ts):
    b, s, _ = z.shape
    cw = c_heads * HEAD_DIM
    n_pairs = c_heads // 2
    c = RWKV_CHUNK
    n_chunks = s // c
    cpb = ts // c
    assert 2 * c == LANES and W_LORA + A_LORA == LANES and G_LORA == 2 * LANES
    col_lo = col0 + 3 * n_pairs
    zero = jnp.zeros((A_LORA, cw), F32)
    wup_ext = jnp.concatenate([w_up, zero], axis=0).astype(BF16)
    aup_ext = jnp.concatenate([zero, a_up], axis=0).astype(BF16)

    def tok(col):
        return pl.BlockSpec((None, ts, LANES), lambda bi, hp, t, col=col: (bi, t, col(hp)))

    cols = [lambda hp: col0 + hp, lambda hp: col0 + n_pairs + hp, lambda hp: col0 + 2 * n_pairs + hp,
            lambda hp: col_lo, lambda hp: col_lo + 1, lambda hp: col_lo + 2]

    def per_pair(rows):
        return pl.BlockSpec((rows, LANES), lambda bi, hp, t: (0, hp))

    mat = lambda dt: jax.ShapeDtypeStruct((b, n_pairs, n_chunks, LANES, LANES), dt)
    mat_spec = pl.BlockSpec((None, None, cpb, LANES, LANES), lambda bi, hp, t: (bi, hp, t, 0, 0))
    tok_out = pl.BlockSpec((None, ts, LANES), lambda bi, hp, t: (bi, t, hp))
    return pl.pallas_call(
        functools.partial(_rwkv_chunk_kernel, ts=ts, unroll=min(16, cpb)),
        grid=(b, n_pairs, s // ts),
        in_specs=[tok(cf) for cf in cols] + [
            per_pair(1), per_pair(LANES), per_pair(1), per_pair(LANES), per_pair(G_LORA),
            per_pair(1), per_pair(1), per_pair(1)],
        out_specs=[mat_spec, mat_spec, mat_spec, mat_spec, tok_out, tok_out],
        out_shape=[mat(BF16), mat(BF16), mat(F32), mat(F32),
                   jax.ShapeDtypeStruct((b, s, cw), F32), jax.ShapeDtypeStruct((b, s, cw), F32)],
        scratch_shapes=[pltpu.VMEM((ts, LANES), F32)] * 2,
        compiler_params=_params("parallel", "parallel", "parallel"),
        name="rwkv_chunks",
    )(z, z, z, z, z, z, w0.reshape(1, cw), wup_ext, a0.reshape(1, cw), aup_ext, g_up.astype(BF16),
      k_k.reshape(1, cw), k_a.reshape(1, cw), r_k.reshape(1, cw))


def _rwkv_state_kernel(mc_ref, qp_ref, nc_ref, y1_ref, bonus_ref, gate_ref, lg_ref, lb_ref, o_ref, h_ref,
                       *, n_pairs, cpb):
    c = RWKV_CHUNK

    @pl.when(pl.program_id(1) == 0)
    def _():
        h_ref[...] = jnp.zeros(h_ref.shape, F32)

    lane = lax.broadcasted_iota(jnp.int32, (c, LANES), 1)
    lo_half = lane < HEAD_DIM
    inv_n = 1.0 / HEAD_DIM

    def pair(g, j, rows):
        cols = slice(g * LANES, (g + 1) * LANES)
        both = _dot(jnp.concatenate([qp_ref[g, j], mc_ref[g, j]], axis=0), h_ref[g].astype(BF16))
        ys = both[:2 * c, :] + y1_ref[g, j]
        h_ref[g] = both[2 * c:, :] + nc_ref[g, j]
        yield
        y = ys[:c, :] + ys[c:, :]
        yc = y - _head_sums(y, lo_half) * inv_n
        yn = yc * lax.rsqrt(_head_sums(yc * yc, lo_half) * inv_n + RWKV_GN_EPS)
        yn = yn * lg_ref[:, cols] + lb_ref[:, cols]
        o_ref[rows, cols] = ((yn + bonus_ref[rows, cols]) * gate_ref[rows, cols]).astype(o_ref.dtype)

    def chunk(j, carry):
        rows = pl.ds(pl.multiple_of(j * c, c), c)
        _run_interleaved([pair(g, j, rows) for g in range(n_pairs)])
        return carry

    lax.fori_loop(0, cpb, chunk, 0)


def rwkv_state_scan(mc, qp, nc, y1, bonus, gate, ln_gain, ln_bias, cpb):
    b, n_pairs, n_chunks = mc.shape[:3]
    s = n_chunks * RWKV_CHUNK
    cw = n_pairs * LANES
    ts = cpb * RWKV_CHUNK
    mat_spec = pl.BlockSpec((None, n_pairs, cpb, LANES, LANES), lambda bi, t: (bi, 0, t, 0, 0))
    tok_spec = pl.BlockSpec((None, ts, cw), lambda bi, t: (bi, t, 0))
    vec_spec = pl.BlockSpec((1, cw), lambda bi, t: (0, 0))
    return pl.pallas_call(
        functools.partial(_rwkv_state_kernel, n_pairs=n_pairs, cpb=cpb),
        grid=(b, n_chunks // cpb),
        in_specs=[mat_spec, mat_spec, mat_spec, mat_spec, tok_spec, tok_spec, vec_spec, vec_spec],
        out_specs=tok_spec,
        out_shape=jax.ShapeDtypeStruct((b, s, cw), BF16),
        scratch_shapes=[pltpu.VMEM((n_pairs, LANES, LANES), F32)],
        compiler_params=_params("parallel", "arbitrary"),
        name="rwkv_state_scan",
    )(mc, qp, nc, y1, bonus, gate, ln_gain.reshape(1, cw), ln_bias.reshape(1, cw))


def _mixer_out_kernel(oa_ref, ob_ref, oc_ref, ag_ref, wa_ref, wb_ref, wc_ref, x_ref, g_ref, o_ref):
    oa = _rms_rows(oa_ref[...], ag_ref[...]).astype(BF16)
    y = _dot(oa, wa_ref[...]) + _dot(ob_ref[...], wb_ref[...]) + _dot(oc_ref[...], wc_ref[...])
    o_ref[...] = x_ref[...] + _rms_rows(y, g_ref[...])


def mixer_out(oa, ob, oc, attn_gain, w_out, x, gain, tm):
    m, d = x.shape
    aw, bw, cw = oa.shape[1], ob.shape[1], oc.shape[1]
    w = w_out.astype(BF16)
    row = lambda width: pl.BlockSpec((tm, width), lambda i: (i, 0))
    full = lambda r, cdim: pl.BlockSpec((r, cdim), lambda i: (0, 0))
    return pl.pallas_call(
        _mixer_out_kernel,
        grid=(m // tm,),
        in_specs=[row(aw), row(bw), row(cw), full(1, aw), full(aw, d), full(bw, d), full(cw, d),
                  row(d), full(1, d)],
        out_specs=row(d),
        out_shape=jax.ShapeDtypeStruct((m, d), F32),
        compiler_params=_params("parallel"),
        name="mixer_out",
    )(oa, ob, oc, attn_gain.reshape(1, aw), w[:aw], w[aw:aw + bw], w[aw + bw:], x, gain.reshape(1, d))


def _cross_attn_kernel(x_ref, gi_ref, wq_ref, kv_ref, wo_ref, go_ref, o_ref, *, heads):
    x = x_ref[...]
    q = _dot(_rms_rows(x, gi_ref[...]).astype(BF16), wq_ref[...])
    inner = heads * MEM_HEAD_DIM
    scale = MEM_HEAD_DIM ** -0.5
    outs = [None] * heads

    def head(h):
        cols = slice(h * MEM_HEAD_DIM, (h + 1) * MEM_HEAD_DIM)
        s = _dot_nt(q[:, cols].astype(BF16), kv_ref[:, cols]) * scale
        yield
        m = jnp.max(s, axis=-1, keepdims=True)
        e = jnp.exp(s - m)
        p = e * (1.0 / jnp.sum(e, axis=-1, keepdims=True))
        outs[h] = _dot(p.astype(BF16), kv_ref[:, inner + h * MEM_HEAD_DIM:inner + (h + 1) * MEM_HEAD_DIM])

    _run_interleaved([head(h) for h in range(heads)])
    o = jnp.concatenate(outs, axis=1).astype(BF16)
    o_ref[...] = x + _rms_rows(_dot(o, wo_ref[...]), go_ref[...])


def cross_attention(x, gain_in, wq, kv, wo, gain_out, seq, tm):
    m, d = x.shape
    inner = wq.shape[1]
    heads = inner // MEM_HEAD_DIM
    mem_tokens = kv.shape[0] // (m // seq)
    blocks_per_seq = seq // tm
    full = lambda r, cdim: pl.BlockSpec((r, cdim), lambda i: (0, 0))
    return pl.pallas_call(
        functools.partial(_cross_attn_kernel, heads=heads),
        grid=(m // tm,),
        in_specs=[pl.BlockSpec((tm, d), lambda i: (i, 0)), full(1, d), full(d, inner),
                  pl.BlockSpec((mem_tokens, 2 * inner), lambda i: (i // blocks_per_seq, 0)),
                  full(inner, d), full(1, d)],
        out_specs=pl.BlockSpec((tm, d), lambda i: (i, 0)),
        out_shape=jax.ShapeDtypeStruct((m, d), F32),
        compiler_params=_params("parallel"),
        name="cross_attention",
    )(x, gain_in.reshape(1, d), wq.astype(BF16), kv, wo.astype(BF16), gain_out.reshape(1, d))


def _conv_ffn_kernel(x_ref, xh_ref, gi_ref, wg_ref, wv_ref, cwg_ref, cwv_ref, cbg_ref, cbv_ref, wd_ref, go_ref,
                     o_ref, h_ref, acc_ref, *, tm, blocks_per_seq, sub):
    i = pl.program_id(0)
    f = pl.program_id(1)

    @pl.when(f == 0)
    def _():
        _norm_rows_with_halo(x_ref, xh_ref, gi_ref, h_ref, i % blocks_per_seq == 0, sub)
        acc_ref[...] = jnp.zeros(acc_ref.shape, F32)

    def conv(w_ref, cw_ref, cb_ref):
        up = _dot(h_ref[...], w_ref[...])
        return (cb_ref[...] + pltpu.roll(up, 2, axis=0) * cw_ref[0:1, :] + pltpu.roll(up, 1, axis=0) * cw_ref[1:2, :]
                + up * cw_ref[2:3, :])

    gate = conv(wg_ref, cwg_ref, cbg_ref)
    val = conv(wv_ref, cwv_ref, cbv_ref)
    act = (_gelu_tanh(gate) * val)[HALO:, :].astype(BF16)
    acc_ref[...] += _dot(act, wd_ref[...])

    @pl.when(f == pl.num_programs(1) - 1)
    def _():
        o_ref[...] = x_ref[...] + _rms_rows(acc_ref[...], go_ref[...])


def conv_ffn(x, gain_in, w_up, conv_w, conv_b, w_down, gain_out, seq, tm, tf):
    m, d = x.shape
    d_ff = w_down.shape[0]
    nf = d_ff // tf
    assert d_ff % tf == 0 and seq % tm == 0 and conv_w.shape[0] == CONV_WIDTH and CONV_WIDTH - 1 <= HALO
    wu = w_up.astype(BF16)
    cb = conv_b.reshape(1, 2 * d_ff)
    sub = min(tm, 256)
    return pl.pallas_call(
        functools.partial(_conv_ffn_kernel, tm=tm, blocks_per_seq=seq // tm, sub=sub),
        grid=(m // tm, nf),
        in_specs=[pl.BlockSpec((tm, d), lambda i, f: (i, 0), pipeline_mode=pl.Buffered(1)),
                  pl.BlockSpec((HALO, d), lambda i, f: (jnp.maximum(i * (tm // HALO) - 1, 0), 0)),
                  pl.BlockSpec((1, d), lambda i, f: (0, 0)),
                  pl.BlockSpec((d, tf), lambda i, f: (0, f)),
                  pl.BlockSpec((d, tf), lambda i, f: (0, nf + f)),
                  pl.BlockSpec((CONV_WIDTH, tf), lambda i, f: (0, f)),
                  pl.BlockSpec((CONV_WIDTH, tf), lambda i, f: (0, nf + f)),
                  pl.BlockSpec((1, tf), lambda i, f: (0, f)),
                  pl.BlockSpec((1, tf), lambda i, f: (0, nf + f)),
                  pl.BlockSpec((tf, d), lambda i, f: (f, 0)),
                  pl.BlockSpec((1, d), lambda i, f: (0, 0))],
        out_specs=pl.BlockSpec((tm, d), lambda i, f: (i, 0), pipeline_mode=pl.Buffered(1)),
        out_shape=jax.ShapeDtypeStruct((m, d), F32),
        scratch_shapes=[pltpu.VMEM((tm + HALO, d), BF16), pltpu.VMEM((tm, d), F32)],
        compiler_params=_params("parallel", "arbitrary"),
        name="conv_ffn",
    )(x, x, gain_in.reshape(1, d), wu, wu, conv_w, conv_w, cb, cb, w_down.astype(BF16), gain_out.reshape(1, d))


def _tile_sizes(seq):
    tm = 512 if seq % 512 == 0 else seq
    return dict(tm_tall=min(1024, seq), tn_proj=768, tm=tm, tf=512, ts=min(1024, seq), cpb=min(16, seq // RWKV_CHUNK))


def kernel(x, mem, rel_bias_table, sandwich_gains, mem_src_gain, w_in, w_out, attn_out_gain, sgu_norm_gain, sgu_w, sgu_b, sgu_out_gain, rwkv_mu, rwkv_w0, rwkv_w_up, rwkv_a0, rwkv_a_up, rwkv_g_up, rwkv_k_k, rwkv_k_a, rwkv_r_k, rwkv_ln_gain, rwkv_ln_bias, mem_wq, mem_wkv, mem_wo, ffn_w_up, ffn_conv_w, ffn_conv_b, ffn_w_down):
    b, s, d = x.shape
    depth = w_in.shape[0]
    a_heads = rel_bias_table.shape[1]
    aw = a_heads * HEAD_DIM
    bw = sgu_norm_gain.shape[1]
    cw = rwkv_w0.shape[1]
    c_heads = cw // HEAD_DIM
    t = _tile_sizes(s)
    m = b * s
    xf = x.reshape(m, d)
    memf = mem.reshape(b * mem.shape[1], d)
    for l in range(depth):
        g = sandwich_gains[l]
        wi = w_in[l]
        n_in = wi.shape[1]
        n_pad = -n_in % t["tn_proj"]
        wi = jnp.concatenate([wi[:, 3 * aw:3 * aw + 2 * bw], wi[:, :3 * aw], wi[:, 3 * aw + 2 * bw:],
                              jnp.zeros((d, n_pad), wi.dtype)], axis=1).astype(BF16)
        col_a = 2 * bw // LANES
        col_c = col_a + 3 * aw // LANES
        mu = jnp.concatenate([jnp.zeros((2 * bw + 3 * aw,), F32), rwkv_mu[l], jnp.zeros((n_pad,), F32)])
        z = in_projection(xf, g[0], wi, mu, s, t["tm_tall"], t["tn_proj"])
        z3 = z.reshape(b, s, n_in + n_pad)
        oa = dilated_attention(z3, col_a, rel_bias_table, a_heads)
        ob = spatial_gating(z, 0, sgu_norm_gain[l], sgu_w[l], sgu_b[l], sgu_out_gain[l], t["tm"])
        mc, qp, nc, y1, bonus, gate = rwkv_chunks(
            z3, col_c, rwkv_w0[l], rwkv_w_up[l], rwkv_a0[l], rwkv_a_up[l], rwkv_g_up[l],
            rwkv_k_k[l], rwkv_k_a[l], rwkv_r_k[l], c_heads, t["ts"])
        oc = rwkv_state_scan(mc, qp, nc, y1, bonus, gate, rwkv_ln_gain[l], rwkv_ln_bias[l], t["cpb"])
        xf = mixer_out(oa.reshape(m, aw), ob, oc.reshape(m, cw), attn_out_gain[l], w_out[l], xf, g[1], t["tm"])
        kv = rms_matmul(memf, mem_src_gain[l], mem_wkv[l].astype(BF16), memf.shape[0], mem_wkv.shape[2] // 2,
                        out_dtype=BF16)
        xf = cross_attention(xf, g[2], mem_wq[l], kv, mem_wo[l], g[3], s, t["tm"])
        xf = conv_ffn(xf, g[4], ffn_w_up[l], ffn_conv_w[l], ffn_conv_b[l], ffn_w_down[l], g[5], s, t["tm_tall"], t["tf"])
    return xf.reshape(b, s, d)
```

```python
import functools
import itertools

import numpy as np
import jax
import jax.numpy as jnp
from jax import lax
from jax.experimental import pallas as pl
from jax.experimental.pallas import tpu as pltpu

HEAD_DIM = 64
LANES = 128
DILATED_PATTERNS = ((128, 1), (512, 4), (2048, 16))
BLOCK = 128
DEINTERLEAVE = 4
REL_BUCKETS = 32
REL_MAX_DISTANCE = 2048
CHUNK = 128
RWKV_CHUNK = 64
W_LORA = 64
A_LORA = 64
G_LORA = 256
RWKV_GN_EPS = 64e-5
MEM_HEAD_DIM = 128
CONV_WIDTH = 3
NORM_EPS = 1e-6
NEG_INF = -1e30
HALO = 16

F32 = jnp.float32
BF16 = jnp.bfloat16
VMEM_LIMIT_BYTES = 56 * 1024 * 1024


def _params(*sem):
    return pltpu.CompilerParams(dimension_semantics=sem, vmem_limit_bytes=VMEM_LIMIT_BYTES)


def _dot(a, b):
    return jnp.dot(a, b, preferred_element_type=F32)


def _dot_nt(a, b):
    return lax.dot_general(a, b, (((1,), (1,)), ((), ())), preferred_element_type=F32)


def _dot_tn(a, b):
    return lax.dot_general(a, b, (((0,), (0,)), ((), ())), preferred_element_type=F32)


def _run_interleaved(stage_generators):
    for _ in itertools.zip_longest(*stage_generators):
        pass


def _rms_rows(x, gain):
    ms = jnp.mean(x * x, axis=-1, keepdims=True)
    return x * lax.rsqrt(ms + NORM_EPS) * gain


def _norm_rows_with_halo(x_ref, xh_ref, g_ref, h_ref, at_seq_start, sub):
    halo = _rms_rows(xh_ref[...], g_ref[...])
    h_ref[0:HALO, :] = jnp.where(at_seq_start, 0.0, halo).astype(BF16)

    def body(t, c):
        rows = pl.ds(pl.multiple_of(t * sub, sub), sub)
        h_ref[pl.ds(pl.multiple_of(HALO + t * sub, HALO), sub), :] = (
            _rms_rows(x_ref[rows, :], g_ref[...]).astype(BF16))
        return c
    lax.fori_loop(0, x_ref.shape[0] // sub, body, 0)


def _gelu_tanh(x):
    return 0.5 * x * (1.0 + jnp.tanh(np.sqrt(2.0 / np.pi).astype(np.float32) * (x + 0.044715 * (x * x * x))))


def _sigmoid(x):
    return 1.0 / (1.0 + jnp.exp(-x))


def _rms_matmul_kernel(x_ref, g_ref, w_ref, o_ref, h_ref, *, sub):
    @pl.when(pl.program_id(1) == 0)
    def _():
        def body(i, c):
            rows = pl.ds(pl.multiple_of(i * sub, sub), sub)
            h_ref[rows, :] = _rms_rows(x_ref[rows, :], g_ref[...]).astype(BF16)
            return c
        lax.fori_loop(0, x_ref.shape[0] // sub, body, 0)

    o_ref[...] = _dot(h_ref[...], w_ref[...]).astype(o_ref.dtype)


def rms_matmul(x, gain, w, tm, tn, out_dtype=F32):
    m, k = x.shape
    n = w.shape[1]
    assert m % tm == 0 and n % tn == 0
    sub = min(tm, 256)
    return pl.pallas_call(
        functools.partial(_rms_matmul_kernel, sub=sub),
        grid=(m // tm, n // tn),
        in_specs=[pl.BlockSpec((tm, k), lambda i, j: (i, 0)),
                  pl.BlockSpec((1, k), lambda i, j: (0, 0)),
                  pl.BlockSpec((k, tn), lambda i, j: (0, j))],
        out_specs=pl.BlockSpec((tm, tn), lambda i, j: (i, j)),
        out_shape=jax.ShapeDtypeStruct((m, n), out_dtype),
        scratch_shapes=[pltpu.VMEM((tm, k), BF16)],
        compiler_params=_params("parallel", "arbitrary"),
        name="rms_matmul",
    )(x, gain.reshape(1, k), w)


def _in_proj_kernel(x_ref, xh_ref, g_ref, w_ref, mu_ref, o_ref, h_ref, *, blocks_per_seq, sub):
    @pl.when(pl.program_id(1) == 0)
    def _():
        _norm_rows_with_halo(x_ref, xh_ref, g_ref, h_ref, pl.program_id(0) % blocks_per_seq == 0, sub)

    z = _dot(h_ref[...], w_ref[...])
    shifted = z + (pltpu.roll(z, 1, axis=0) - z) * mu_ref[...]
    o_ref[...] = shifted[HALO:, :]


def in_projection(x, gain, w, mu, seq, tm, tn):
    m, k = x.shape
    n = w.shape[1]
    assert m % tm == 0 and n % tn == 0 and seq % tm == 0
    sub = min(tm, 256)
    return pl.pallas_call(
        functools.partial(_in_proj_kernel, blocks_per_seq=seq // tm, sub=sub),
        grid=(m // tm, n // tn),
        in_specs=[pl.BlockSpec((tm, k), lambda i, j: (i, 0)),
                  pl.BlockSpec((HALO, k), lambda i, j: (jnp.maximum(i * (tm // HALO) - 1, 0), 0)),
                  pl.BlockSpec((1, k), lambda i, j: (0, 0)),
                  pl.BlockSpec((k, tn), lambda i, j: (0, j)),
                  pl.BlockSpec((1, tn), lambda i, j: (0, j))],
        out_specs=pl.BlockSpec((tm, tn), lambda i, j: (i, j)),
        out_shape=jax.ShapeDtypeStruct((m, n), F32),
        scratch_shapes=[pltpu.VMEM((tm + HALO, k), BF16)],
        compiler_params=_params("parallel", "arbitrary"),
        name="in_projection",
    )(x, x, gain.reshape(1, k), w, mu.reshape(1, n))


def _t5_causal_bucket(dist):
    max_exact = REL_BUCKETS // 2
    d = np.maximum(dist, 0)
    scaled = np.log(np.maximum(d, 1) / max_exact) / np.log(REL_MAX_DISTANCE / max_exact)
    large = np.minimum(max_exact + (scaled * (REL_BUCKETS - max_exact)).astype(np.int32), REL_BUCKETS - 1)
    return np.where(d < max_exact, d, large).astype(np.int32)


def _bucket_tables():
    out = []
    qi = np.arange(BLOCK)[:, None]
    kj = np.arange(2 * BLOCK)[None, :]
    for window, dilation in DILATED_PATTERNS:
        steps = window // dilation
        assert steps <= BLOCK
        delta = qi + BLOCK - kj
        band = (delta >= 0) & (delta <= steps)
        bucket = _t5_causal_bucket(np.clip(delta, 0, steps) * dilation)
        out.append(np.where(band, bucket, -1))
    return np.stack(out).astype(np.int32)


def _dilated_attn_kernel(table_ref, bucket_ref, q_ref, k_ref, v_ref, o_ref, bias_ref, acc_ref, lse_ref,
                         q4_ref, k4_ref, v4_ref, *, seq, n_pairs, unroll):
    hp = pl.program_id(0)
    quarter = seq // DEINTERLEAVE
    blocks_per_residue = quarter // BLOCK

    def token_rows(i):
        r4 = i // blocks_per_residue
        blk = i % blocks_per_residue
        return pl.ds(r4 + blk * (DEINTERLEAVE * BLOCK), BLOCK, stride=DEINTERLEAVE)

    def deinterleave(i, carry):
        dst = pl.ds(pl.multiple_of(i * BLOCK, BLOCK), BLOCK)
        src = token_rows(i)
        q4_ref[dst, :] = q_ref[src, :]
        k4_ref[dst, :] = k_ref[src, :]
        v4_ref[dst, :] = v_ref[src, :]
        return carry

    lax.fori_loop(0, seq // BLOCK, deinterleave, 0)
    lane = lax.broadcasted_iota(jnp.int32, (BLOCK, LANES), 1)
    lo_half = lane < HEAD_DIM

    @pl.when(pl.program_id(1) == 0)
    def _():
        key = lax.broadcasted_iota(jnp.int32, (BLOCK, 2 * BLOCK), 1)
        for p in range(len(DILATED_PATTERNS)):
            bucket = bucket_ref[p]
            for h in range(2):
                val = jnp.zeros(bucket.shape, F32)
                for b in range(REL_BUCKETS):
                    val = jnp.where(bucket == b, table_ref[b, 2 * hp + h], val)
                val = jnp.where(bucket < 0, NEG_INF, val)
                bias_ref[p, h, 0] = val
                bias_ref[p, h, 1] = jnp.where(key < BLOCK, NEG_INF, val)

    scale = HEAD_DIM ** -0.5
    for p, (window, d) in enumerate(DILATED_PATTERNS):
        n_blk = seq // (d * BLOCK)
        shift = int(np.log2(d))

        def block(idx, p=p, d=d, shift=shift):
            r = idx & (d - 1)
            n = idx >> shift
            n_prev = jnp.maximum(n - 1, 0)
            if d == 1:
                qs, ks, vs = q_ref, k_ref, v_ref
                cur_rows = pl.ds(pl.multiple_of(n * BLOCK, BLOCK), BLOCK)
                prev_rows = pl.ds(pl.multiple_of(n_prev * BLOCK, BLOCK), BLOCK)
            elif d == DEINTERLEAVE:
                qs, ks, vs = q4_ref, k4_ref, v4_ref
                cur_rows = pl.ds(pl.multiple_of(r * quarter + n * BLOCK, BLOCK), BLOCK)
                prev_rows = pl.ds(pl.multiple_of(r * quarter + n_prev * BLOCK, BLOCK), BLOCK)
            else:
                qs, ks, vs = q4_ref, k4_ref, v4_ref
                sub = d // DEINTERLEAVE
                base = (r % DEINTERLEAVE) * quarter + r // DEINTERLEAVE
                cur_rows = pl.ds(base + n * (sub * BLOCK), BLOCK, stride=sub)
                prev_rows = pl.ds(base + n_prev * (sub * BLOCK), BLOCK, stride=sub)
            q = qs[cur_rows, :] * scale
            kcat = jnp.concatenate([ks[prev_rows, :], ks[cur_rows, :]], axis=0).astype(BF16)
            vcat = jnp.concatenate([vs[prev_rows, :], vs[cur_rows, :]], axis=0).astype(BF16)
            var = jnp.where(n == 0, 1, 0)
            scores = [_dot_nt(jnp.where(lo_half if h == 0 else ~lo_half, q, 0.0).astype(BF16), kcat)
                      for h in range(2)]
            yield
            outs, lses, dens = [], [], []
            for h in range(2):
                s = scores[h] + bias_ref[p, h, var]
                m = jnp.max(s, axis=-1, keepdims=True)
                e = jnp.exp(s - m)
                den = jnp.sum(e, axis=-1, keepdims=True)
                outs.append(_dot(e.astype(BF16), vcat))
                dens.append(den)
                lses.append(m + jnp.log(den))
            yield
            acc_ref[p, cur_rows, :] = jnp.where(lo_half, outs[0] * (1.0 / dens[0]), outs[1] * (1.0 / dens[1]))
            lse_ref[p, cur_rows, :] = jnp.where(lo_half, lses[0], lses[1])

        def block_group(i, carry, block=block):
            _run_interleaved([block(i * unroll + u) for u in range(unroll)])
            return carry

        lax.fori_loop(0, d * n_blk // unroll, block_group, 0)

    def merge(i, carry):
        rows = pl.ds(pl.multiple_of(i * BLOCK, BLOCK), BLOCK)
        tok = token_rows(i)
        l0, l1, l2 = lse_ref[0, tok, :], lse_ref[1, rows, :], lse_ref[2, rows, :]
        mx = jnp.maximum(jnp.maximum(l0, l1), l2)
        w0, w1, w2 = jnp.exp(l0 - mx), jnp.exp(l1 - mx), jnp.exp(l2 - mx)
        num = w0 * acc_ref[0, tok, :] + w1 * acc_ref[1, rows, :] + w2 * acc_ref[2, rows, :]
        o_ref[tok, :] = num / (w0 + w1 + w2)
        return carry

    lax.fori_loop(0, seq // BLOCK, merge, 0)


def dilated_attention(z, col0, rel_bias_table, a_heads):
    b, s, _ = z.shape
    n_pairs = a_heads // 2
    unroll = 16
    assert s % (DILATED_PATTERNS[-1][1] * BLOCK) == 0 and (s // BLOCK) % unroll == 0
    assert [d for _, d in DILATED_PATTERNS] == [1, DEINTERLEAVE, DEINTERLEAVE ** 2]
    buckets = jnp.asarray(_bucket_tables())
    n_pat = len(DILATED_PATTERNS)
    return pl.pallas_call(
        functools.partial(_dilated_attn_kernel, seq=s, n_pairs=n_pairs, unroll=unroll),
        grid=(n_pairs, b),
        in_specs=[pl.BlockSpec(memory_space=pltpu.SMEM),
                  pl.BlockSpec((n_pat, BLOCK, 2 * BLOCK), lambda hp, bi: (0, 0, 0)),
                  pl.BlockSpec((None, s, LANES), lambda hp, bi: (bi, 0, col0 + hp)),
                  pl.BlockSpec((None, s, LANES), lambda hp, bi: (bi, 0, col0 + n_pairs + hp)),
                  pl.BlockSpec((None, s, LANES), lambda hp, bi: (bi, 0, col0 + 2 * n_pairs + hp))],
        out_specs=pl.BlockSpec((None, s, LANES), lambda hp, bi: (bi, 0, hp)),
        out_shape=jax.ShapeDtypeStruct((b, s, a_heads * HEAD_DIM), F32),
        scratch_shapes=[pltpu.VMEM((n_pat, 2, 2, BLOCK, 2 * BLOCK), F32),
                        pltpu.VMEM((n_pat, s, LANES), F32),
                        pltpu.VMEM((n_pat, s, LANES), F32)] + [pltpu.VMEM((s, LANES), F32)] * 3,
        compiler_params=_params("parallel", "arbitrary"),
        name="dilated_attention",
    )(rel_bias_table, buckets, z, z, z)


def _sgu_kernel(z_ref, ng_ref, w_ref, b_ref, og_ref, o_ref, *, width, groups):
    z = _gelu_tanh(z_ref[...])
    u = z[:, :width]
    g = z[:, width:]
    gc = g - jnp.mean(g, axis=-1, keepdims=True)
    gn = gc * lax.rsqrt(jnp.mean(gc * gc, axis=-1, keepdims=True) + NORM_EPS) * ng_ref[...]
    ri = lax.broadcasted_iota(jnp.int32, (CHUNK, CHUNK), 0)
    ci = lax.broadcasted_iota(jnp.int32, (CHUNK, CHUNK), 1)
    causal = ri >= ci
    lane = lax.broadcasted_iota(jnp.int32, (CHUNK, LANES), 1)
    lo_half = lane < HEAD_DIM
    rows = z.shape[0]
    cols = []
    for j in range(groups // 2):
        w0 = jnp.where(causal, w_ref[2 * j], 0.0).astype(BF16)
        w1 = jnp.where(causal, w_ref[2 * j + 1], 0.0).astype(BF16)
        parts = []
        for c in range(rows // CHUNK):
            blk = gn[c * CHUNK:(c + 1) * CHUNK, j * LANES:(j + 1) * LANES]
            g0 = jnp.where(lo_half, blk, 0.0).astype(BF16)
            g1 = jnp.where(lo_half, 0.0, blk).astype(BF16)
            parts.append(_dot(w0, g0) + _dot(w1, g1) + b_ref[:, j * LANES:(j + 1) * LANES])
        cols.append(jnp.concatenate(parts, axis=0))
    mixed = jnp.concatenate(cols, axis=1)
    o_ref[...] = _rms_rows(u * mixed, og_ref[...]).astype(o_ref.dtype)


def spatial_gating(z, col_blk, norm_gain, w_s, b_s, out_gain, tm):
    m = z.shape[0]
    width = norm_gain.shape[0]
    two_w = 2 * width
    groups = w_s.shape[0]
    bias = jnp.repeat(b_s.T, HEAD_DIM, axis=1)
    return pl.pallas_call(
        functools.partial(_sgu_kernel, width=width, groups=groups),
        grid=(m // tm,),
        in_specs=[pl.BlockSpec((tm, two_w), lambda i: (i, col_blk)),
                  pl.BlockSpec((1, width), lambda i: (0, 0)),
                  pl.BlockSpec((groups, CHUNK, CHUNK), lambda i: (0, 0, 0)),
                  pl.BlockSpec((CHUNK, width), lambda i: (0, 0)),
                  pl.BlockSpec((1, width), lambda i: (0, 0))],
        out_specs=pl.BlockSpec((tm, width), lambda i: (i, 0)),
        out_shape=jax.ShapeDtypeStruct((m, width), BF16),
        compiler_params=_params("parallel"),
        name="spatial_gating",
    )(z, norm_gain.reshape(1, width), w_s, bias, out_gain.reshape(1, width))


def _head_sums(z, lo_half):
    s0 = jnp.sum(jnp.where(lo_half, z, 0.0), axis=-1, keepdims=True)
    s1 = jnp.sum(jnp.where(lo_half, 0.0, z), axis=-1, keepdims=True)
    return jnp.where(lo_half, s0, s1)


def _stack_heads(z, lo_half):
    return jnp.concatenate([jnp.where(lo_half, z, 0.0), jnp.where(lo_half, 0.0, z)], axis=0)


def _rwkv_chunk_kernel(r_ref, k_ref, v_ref, lo_ref, g0_ref, g1_ref,
                       w0_ref, wup_ref, a0_ref, aup_ref, gup_ref, kk_ref, ka_ref, rk_ref,
                       mc_ref, qp_ref, nc_ref, y1_ref, bonus_ref, gate_ref, e_ref, a_ref, *, ts, unroll):
    c = RWKV_CHUNK

    lo = lo_ref[...]
    z = w0_ref[...] + _dot(jnp.tanh(lo).astype(BF16), wup_ref[...])
    w_log = -(jnp.maximum(-z, 0.0) + jnp.log(1.0 + jnp.exp(-jnp.abs(z)))) - 0.5
    e_ref[...] = jnp.exp(w_log)
    a_ref[...] = _sigmoid(a0_ref[...] + _dot(lo.astype(BF16), aup_ref[...]))
    gl = jnp.concatenate([g0_ref[...], g1_ref[...]], axis=1)
    gate_ref[...] = _dot(_sigmoid(gl).astype(BF16), gup_ref[...])
    lane = lax.broadcasted_iota(jnp.int32, (c, LANES), 1)
    lo_half = lane < HEAD_DIM
    trow = lax.broadcasted_iota(jnp.int32, (c, LANES), 0)
    ri = lax.broadcasted_iota(jnp.int32, (2 * c, 2 * c), 0)
    ci = lax.broadcasted_iota(jnp.int32, (2 * c, 2 * c), 1)
    same = (ri // c) == (ci // c)
    strict = same & ((ri % c) > (ci % c))
    incl = same & ((ri % c) >= (ci % c))
    diag = ri == ci

    def chunk(j):
        rows = pl.ds(pl.multiple_of(j * c, c), c)
        r = r_ref[rows, :]
        kx = k_ref[rows, :]
        v = v_ref[rows, :]
        e = e_ref[rows, :]
        a = a_ref[rows, :]
        kk = kx * kk_ref[...]
        kk = kk / jnp.maximum(jnp.sqrt(_head_sums(kk * kk, lo_half)), 1e-12)
        k = kx * (1.0 + (a - 1.0) * ka_ref[...])
        bonus_ref[rows, :] = _head_sums(r * k * rk_ref[...], lo_half) * v

        cs = e
        for sh in (1, 2, 4, 8, 16, 32):
            cs = cs + jnp.where(trow >= sh, pltpu.roll(cs, sh, axis=0), 0.0)
        cs_end = cs[c - 1:c, :]
        p = jnp.exp(-cs)
        p_before = jnp.exp(e - cs)
        p_inv = jnp.exp(cs)
        p_rest = jnp.exp(cs - cs_end)
        be = kk * a
        at = _stack_heads(-kk * p_before, lo_half)
        rt = _stack_heads(r * p, lo_half)
        bt = _stack_heads(be * p_inv, lo_half).astype(BF16)
        kt = _stack_heads(k * p_inv, lo_half).astype(BF16)
        bh = _stack_heads(be * p_rest, lo_half).astype(BF16)
        kh = _stack_heads(k * p_rest, lo_half).astype(BF16)
        vs = _stack_heads(v, lo_half).astype(BF16)

        cc = _dot_nt(jnp.concatenate([at, rt], axis=0).astype(BF16), jnp.concatenate([bt, kt], axis=0))
        yield
        l_ab = jnp.where(strict, cc[:2 * c, :2 * c], 0.0)
        l_ak = jnp.where(strict, cc[:2 * c, 2 * c:], 0.0)
        m_rb = jnp.where(incl, cc[2 * c:, :2 * c], 0.0)
        m_rk = jnp.where(incl, cc[2 * c:, 2 * c:], 0.0)

        lv = _dot(jnp.concatenate([l_ak, m_rk], axis=0).astype(BF16), vs)
        x = jnp.concatenate([at, lv[:2 * c, :]], axis=1)
        yield
        lp = l_ab.astype(BF16)
        n = 1
        while n < c:
            x = x + _dot(lp, x.astype(BF16))
            yield
            n *= 2
            if n < c:
                lp = _dot(lp, lp).astype(BF16)
        x16 = x.astype(BF16)

        mw = _dot(m_rb.astype(BF16), x16)
        yield
        qp_ref[j] = (rt + mw[:, :LANES]).astype(BF16)
        y1_ref[j] = mw[:, LANES:] + lv[2 * c:, :]
        bw = _dot_tn(bh, x16)
        p_end = jnp.exp(-cs_end)
        decay_end = jnp.where(diag, jnp.broadcast_to(p_end, (2 * c, LANES)), 0.0)
        mc_ref[j] = (decay_end + bw[:, :LANES]).astype(BF16)
        nc_ref[j] = bw[:, LANES:] + _dot_tn(kh, vs)

    def chunk_group(i, carry):
        _run_interleaved([chunk(i * unroll + u) for u in range(unroll)])
        return carry

    lax.fori_loop(0, ts // (c * unroll), chunk_group, 0)


def rwkv_chunks(z, col0, w0, w_up, a0, a_up, g_up, k_k, k_a, r_k, c_heads, ts):
    b, s, _ = z.shape
    cw = c_heads * HEAD_DIM
    n_pairs = c_heads // 2
    c = RWKV_CHUNK
    n_chunks = s // c
    cpb = ts // c
    assert 2 * c == LANES and W_LORA + A_LORA == LANES and G_LORA == 2 * LANES
    col_lo = col0 + 3 * n_pairs
    zero = jnp.zeros((A_LORA, cw), F32)
    wup_ext = jnp.concatenate([w_up, zero], axis=0).astype(BF16)
    aup_ext = jnp.concatenate([zero, a_up], axis=0).astype(BF16)

    def tok(col):
        return pl.BlockSpec((None, ts, LANES), lambda bi, hp, t, col=col: (bi, t, col(hp)))

    cols = [lambda hp: col0 + hp, lambda hp: col0 + n_pairs + hp, lambda hp: col0 + 2 * n_pairs + hp,
            lambda hp: col_lo, lambda hp: col_lo + 1, lambda hp: col_lo + 2]

    def per_pair(rows):
        return pl.BlockSpec((rows, LANES), lambda bi, hp, t: (0, hp))

    mat = lambda dt: jax.ShapeDtypeStruct((b, n_pairs, n_chunks, LANES, LANES), dt)
    mat_spec = pl.BlockSpec((None, None, cpb, LANES, LANES), lambda bi, hp, t: (bi, hp, t, 0, 0))
    tok_out = pl.BlockSpec((None, ts, LANES), lambda bi, hp, t: (bi, t, hp))
    return pl.pallas_call(
        functools.partial(_rwkv_chunk_kernel, ts=ts, unroll=min(16, cpb)),
        grid=(b, n_pairs, s // ts),
        in_specs=[tok(cf) for cf in cols] + [
            per_pair(1), per_pair(LANES), per_pair(1), per_pair(LANES), per_pair(G_LORA),
            per_pair(1), per_pair(1), per_pair(1)],
        out_specs=[mat_spec, mat_spec, mat_spec, mat_spec, tok_out, tok_out],
        out_shape=[mat(BF16), mat(BF16), mat(F32), mat(F32),
                   jax.ShapeDtypeStruct((b, s, cw), F32), jax.ShapeDtypeStruct((b, s, cw), F32)],
        scratch_shapes=[pltpu.VMEM((ts, LANES), F32)] * 2,
        compiler_params=_params("parallel", "parallel", "parallel"),
        name="rwkv_chunks",
    )(z, z, z, z, z, z, w0.reshape(1, cw), wup_ext, a0.reshape(1, cw), aup_ext, g_up.astype(BF16),
      k_k.reshape(1, cw), k_a.reshape(1, cw), r_k.reshape(1, cw))


def _rwkv_state_kernel(mc_ref, qp_ref, nc_ref, y1_ref, bonus_ref, gate_ref, lg_ref, lb_ref, o_ref, h_ref,
                       *, n_pairs, cpb):
    c = RWKV_CHUNK

    @pl.when(pl.program_id(1) == 0)
    def _():
        h_ref[...] = jnp.zeros(h_ref.shape, F32)

    lane = lax.broadcasted_iota(jnp.int32, (c, LANES), 1)
    lo_half = lane < HEAD_DIM
    inv_n = 1.0 / HEAD_DIM

    def pair(g, j, rows):
        cols = slice(g * LANES, (g + 1) * LANES)
        both = _dot(jnp.concatenate([qp_ref[g, j], mc_ref[g, j]], axis=0), h_ref[g].astype(BF16))
        ys = both[:2 * c, :] + y1_ref[g, j]
        h_ref[g] = both[2 * c:, :] + nc_ref[g, j]
        yield
        y = ys[:c, :] + ys[c:, :]
        yc = y - _head_sums(y, lo_half) * inv_n
        yn = yc * lax.rsqrt(_head_sums(yc * yc, lo_half) * inv_n + RWKV_GN_EPS)
        yn = yn * lg_ref[:, cols] + lb_ref[:, cols]
        o_ref[rows, cols] = ((yn + bonus_ref[rows, cols]) * gate_ref[rows, cols]).astype(o_ref.dtype)

    def chunk(j, carry):
        rows = pl.ds(pl.multiple_of(j * c, c), c)
        _run_interleaved([pair(g, j, rows) for g in range(n_pairs)])
        return carry

    lax.fori_loop(0, cpb, chunk, 0)


def rwkv_state_scan(mc, qp, nc, y1, bonus, gate, ln_gain, ln_bias, cpb):
    b, n_pairs, n_chunks = mc.shape[:3]
    s = n_chunks * RWKV_CHUNK
    cw = n_pairs * LANES
    ts = cpb * RWKV_CHUNK
    mat_spec = pl.BlockSpec((None, n_pairs, cpb, LANES, LANES), lambda bi, t: (bi, 0, t, 0, 0))
    tok_spec = pl.BlockSpec((None, ts, cw), lambda bi, t: (bi, t, 0))
    vec_spec = pl.BlockSpec((1, cw), lambda bi, t: (0, 0))
    return pl.pallas_call(
        functools.partial(_rwkv_state_kernel, n_pairs=n_pairs, cpb=cpb),
        grid=(b, n_chunks // cpb),
        in_specs=[mat_spec, mat_spec, mat_spec, mat_spec, tok_spec, tok_spec, vec_spec, vec_spec],
        out_specs=tok_spec,
        out_shape=jax.ShapeDtypeStruct((b, s, cw), BF16),
        scratch_shapes=[pltpu.VMEM((n_pairs, LANES, LANES), F32)],
        compiler_params=_params("parallel", "arbitrary"),
        name="rwkv_state_scan",
    )(mc, qp, nc, y1, bonus, gate, ln_gain.reshape(1, cw), ln_bias.reshape(1, cw))


def _mixer_out_kernel(oa_ref, ob_ref, oc_ref, ag_ref, wa_ref, wb_ref, wc_ref, x_ref, g_ref, o_ref):
    oa = _rms_rows(oa_ref[...], ag_ref[...]).astype(BF16)
    y = _dot(oa, wa_ref[...]) + _dot(ob_ref[...], wb_ref[...]) + _dot(oc_ref[...], wc_ref[...])
    o_ref[...] = x_ref[...] + _rms_rows(y, g_ref[...])


def mixer_out(oa, ob, oc, attn_gain, w_out, x, gain, tm):
    m, d = x.shape
    aw, bw, cw = oa.shape[1], ob.shape[1], oc.shape[1]
    w = w_out.astype(BF16)
    row = lambda width: pl.BlockSpec((tm, width), lambda i: (i, 0))
    full = lambda r, cdim: pl.BlockSpec((r, cdim), lambda i: (0, 0))
    return pl.pallas_call(
        _mixer_out_kernel,
        grid=(m // tm,),
        in_specs=[row(aw), row(bw), row(cw), full(1, aw), full(aw, d), full(bw, d), full(cw, d),
                  row(d), full(1, d)],
        out_specs=row(d),
        out_shape=jax.ShapeDtypeStruct((m, d), F32),
        compiler_params=_params("parallel"),
        name="mixer_out",
    )(oa, ob, oc, attn_gain.reshape(1, aw), w[:aw], w[aw:aw + bw], w[aw + bw:], x, gain.reshape(1, d))


def _cross_attn_kernel(x_ref, gi_ref, wq_ref, kv_ref, wo_ref, go_ref, gn_ref, o_ref, hn_ref, *, heads):
    x = x_ref[...]
    q = _dot(_rms_rows(x, gi_ref[...]).astype(BF16), wq_ref[...])
    inner = heads * MEM_HEAD_DIM
    scale = MEM_HEAD_DIM ** -0.5
    outs = [None] * heads

    def head(h):
        cols = slice(h * MEM_HEAD_DIM, (h + 1) * MEM_HEAD_DIM)
        s = _dot_nt(q[:, cols].astype(BF16), kv_ref[:, cols]) * scale
        yield
        m = jnp.max(s, axis=-1, keepdims=True)
        e = jnp.exp(s - m)
        p = e * (1.0 / jnp.sum(e, axis=-1, keepdims=True))
        outs[h] = _dot(p.astype(BF16), kv_ref[:, inner + h * MEM_HEAD_DIM:inner + (h + 1) * MEM_HEAD_DIM])

    _run_interleaved([head(h) for h in range(heads)])
    o = jnp.concatenate(outs, axis=1).astype(BF16)
    x_out = x + _rms_rows(_dot(o, wo_ref[...]), go_ref[...])
    o_ref[...] = x_out
    hn_ref[...] = _rms_rows(x_out, gn_ref[...]).astype(hn_ref.dtype)


def cross_attention(x, gain_in, wq, kv, wo, gain_out, gain_next, seq, tm):
    m, d = x.shape
    inner = wq.shape[1]
    heads = inner // MEM_HEAD_DIM
    mem_tokens = kv.shape[0] // (m // seq)
    blocks_per_seq = seq // tm
    full = lambda r, cdim: pl.BlockSpec((r, cdim), lambda i: (0, 0))
    return pl.pallas_call(
        functools.partial(_cross_attn_kernel, heads=heads),
        grid=(m // tm,),
        in_specs=[pl.BlockSpec((tm, d), lambda i: (i, 0)), full(1, d), full(d, inner),
                  pl.BlockSpec((mem_tokens, 2 * inner), lambda i: (i // blocks_per_seq, 0)),
                  full(inner, d), full(1, d), full(1, d)],
        out_specs=[pl.BlockSpec((tm, d), lambda i: (i, 0)), pl.BlockSpec((tm, d), lambda i: (i, 0))],
        out_shape=[jax.ShapeDtypeStruct((m, d), F32), jax.ShapeDtypeStruct((m, d), BF16)],
        compiler_params=_params("parallel"),
        name="cross_attention",
    )(x, gain_in.reshape(1, d), wq.astype(BF16), kv, wo.astype(BF16), gain_out.reshape(1, d),
      gain_next.reshape(1, d))


def _conv_ffn_kernel(h_ref, hh_ref, wg_ref, wv_ref, cwg_ref, cwv_ref, cbg_ref, cbv_ref, wd_ref, x_ref, go_ref,
                     o_ref, hs_ref, acc_ref, inv_ref, *, blocks_per_seq, nf, n_col):
    i = pl.program_id(0)
    f = pl.program_id(1)
    tc = o_ref.shape[1]

    @pl.when(f == 0)
    def _():
        hs_ref[0:HALO, :] = jnp.where(i % blocks_per_seq == 0, jnp.zeros_like(hh_ref[...]), hh_ref[...])
        hs_ref[HALO:, :] = h_ref[...]
        acc_ref[...] = jnp.zeros(acc_ref.shape, F32)

    @pl.when(f < nf)
    def _():
        def conv(w_ref, cw_ref, cb_ref):
            up = _dot(hs_ref[...], w_ref[...])
            return (cb_ref[...] + pltpu.roll(up, 2, axis=0) * cw_ref[0:1, :]
                    + pltpu.roll(up, 1, axis=0) * cw_ref[1:2, :] + up * cw_ref[2:3, :])

        gate = conv(wg_ref, cwg_ref, cbg_ref)
        val = conv(wv_ref, cwv_ref, cbv_ref)
        act = (_gelu_tanh(gate) * val)[HALO:, :].astype(BF16)
        for c in range(n_col):
            acc_ref[c] += _dot(act, wd_ref[:, c * tc:(c + 1) * tc])

    @pl.when(f == nf)
    def _():
        ss = jnp.zeros((acc_ref.shape[1], 1), F32)
        for c in range(n_col):
            y = acc_ref[c]
            ss = ss + jnp.sum(y * y, axis=-1, keepdims=True)
        inv_ref[...] = lax.rsqrt(ss * (1.0 / (n_col * tc)) + NORM_EPS)

    @pl.when(f >= nf)
    def _():
        o_ref[...] = x_ref[...] + acc_ref[f - nf] * inv_ref[...] * go_ref[...]


def conv_ffn(h, x, conv_w, conv_b, w_up, w_down, gain_out, seq, tm, tf, tc):
    m, d = x.shape
    d_ff = w_down.shape[0]
    nf = d_ff // tf
    n_col = d // tc
    assert d_ff % tf == 0 and d % tc == 0 and seq % tm == 0
    assert conv_w.shape[0] == CONV_WIDTH and CONV_WIDTH - 1 <= HALO
    wu = w_up.astype(BF16)
    cb = conv_b.reshape(1, 2 * d_ff)
    up_blk = lambda f: jnp.minimum(f, nf - 1)
    col_blk = lambda f: jnp.maximum(f - nf, 0)
    return pl.pallas_call(
        functools.partial(_conv_ffn_kernel, blocks_per_seq=seq // tm, nf=nf, n_col=n_col),
        grid=(m // tm, nf + n_col),
        in_specs=[pl.BlockSpec((tm, d), lambda i, f: (i, 0)),
                  pl.BlockSpec((HALO, d), lambda i, f: (jnp.maximum(i * (tm // HALO) - 1, 0), 0)),
                  pl.BlockSpec((d, tf), lambda i, f: (0, up_blk(f))),
                  pl.BlockSpec((d, tf), lambda i, f: (0, nf + up_blk(f))),
                  pl.BlockSpec((CONV_WIDTH, tf), lambda i, f: (0, up_blk(f))),
                  pl.BlockSpec((CONV_WIDTH, tf), lambda i, f: (0, nf + up_blk(f))),
                  pl.BlockSpec((1, tf), lambda i, f: (0, up_blk(f))),
                  pl.BlockSpec((1, tf), lambda i, f: (0, nf + up_blk(f))),
                  pl.BlockSpec((tf, d), lambda i, f: (up_blk(f), 0)),
                  pl.BlockSpec((tm, tc), lambda i, f: (i, col_blk(f))),
                  pl.BlockSpec((1, tc), lambda i, f: (0, col_blk(f)))],
        out_specs=pl.BlockSpec((tm, tc), lambda i, f: (i, col_blk(f))),
        out_shape=jax.ShapeDtypeStruct((m, d), F32),
        scratch_shapes=[pltpu.VMEM((tm + HALO, d), BF16), pltpu.VMEM((n_col, tm, tc), F32),
                        pltpu.VMEM((tm, 1), F32)],
        compiler_params=_params("parallel", "arbitrary"),
        name="conv_ffn",
    )(h, h, wu, wu, conv_w, conv_w, cb, cb, w_down.astype(BF16), x, gain_out.reshape(1, d))


def _tile_sizes(seq):
    tm = 512 if seq % 512 == 0 else seq
    return dict(tm_tall=min(1024, seq), tn_proj=768, tm=tm, tf=512, tc=512, ts=min(1024, seq), cpb=min(16, seq // RWKV_CHUNK))


def kernel(x, mem, rel_bias_table, sandwich_gains, mem_src_gain, w_in, w_out, attn_out_gain, sgu_norm_gain, sgu_w, sgu_b, sgu_out_gain, rwkv_mu, rwkv_w0, rwkv_w_up, rwkv_a0, rwkv_a_up, rwkv_g_up, rwkv_k_k, rwkv_k_a, rwkv_r_k, rwkv_ln_gain, rwkv_ln_bias, mem_wq, mem_wkv, mem_wo, ffn_w_up, ffn_conv_w, ffn_conv_b, ffn_w_down):
    b, s, d = x.shape
    depth = w_in.shape[0]
    a_heads = rel_bias_table.shape[1]
    aw = a_heads * HEAD_DIM
    bw = sgu_norm_gain.shape[1]
    cw = rwkv_w0.shape[1]
    c_heads = cw // HEAD_DIM
    t = _tile_sizes(s)
    m = b * s
    xf = x.reshape(m, d)
    memf = mem.reshape(b * mem.shape[1], d)
    for l in range(depth):
        g = sandwich_gains[l]
        wi = w_in[l]
        n_in = wi.shape[1]
        n_pad = -n_in % t["tn_proj"]
        wi = jnp.concatenate([wi[:, 3 * aw:3 * aw + 2 * bw], wi[:, :3 * aw], wi[:, 3 * aw + 2 * bw:],
                              jnp.zeros((d, n_pad), wi.dtype)], axis=1).astype(BF16)
        col_a = 2 * bw // LANES
        col_c = col_a + 3 * aw // LANES
        mu = jnp.concatenate([jnp.zeros((2 * bw + 3 * aw,), F32), rwkv_mu[l], jnp.zeros((n_pad,), F32)])
        z = in_projection(xf, g[0], wi, mu, s, t["tm_tall"], t["tn_proj"])
        z3 = z.reshape(b, s, n_in + n_pad)
        oa = dilated_attention(z3, col_a, rel_bias_table, a_heads)
        ob = spatial_gating(z, 0, sgu_norm_gain[l], sgu_w[l], sgu_b[l], sgu_out_gain[l], t["tm"])
        mc, qp, nc, y1, bonus, gate = rwkv_chunks(
            z3, col_c, rwkv_w0[l], rwkv_w_up[l], rwkv_a0[l], rwkv_a_up[l], rwkv_g_up[l],
            rwkv_k_k[l], rwkv_k_a[l], rwkv_r_k[l], c_heads, t["ts"])
        oc = rwkv_state_scan(mc, qp, nc, y1, bonus, gate, rwkv_ln_gain[l], rwkv_ln_bias[l], t["cpb"])
        xf = mixer_out(oa.reshape(m, aw), ob, oc.reshape(m, cw), attn_out_gain[l], w_out[l], xf, g[1], t["tm"])
        kv = rms_matmul(memf, mem_src_gain[l], mem_wkv[l].astype(BF16), memf.shape[0], mem_wkv.shape[2] // 2,
                        out_dtype=BF16)
        xf, hf = cross_attention(xf, g[2], mem_wq[l], kv, mem_wo[l], g[3], g[4], s, t["tm"])
        xf = conv_ffn(hf, xf, ffn_conv_w[l], ffn_conv_b[l], ffn_w_up[l], ffn_w_down[l], g[5], s,
                      t["tm_tall"], t["tf"], t["tc"])
    return xf.reshape(b, s, d)
```

```python
import functools
import itertools

import numpy as np
import jax
import jax.numpy as jnp
from jax import lax
from jax.experimental import pallas as pl
from jax.experimental.pallas import tpu as pltpu

HEAD_DIM = 64
LANES = 128
DILATED_PATTERNS = ((128, 1), (512, 4), (2048, 16))
BLOCK = 128
DEINTERLEAVE = 4
REL_BUCKETS = 32
REL_MAX_DISTANCE = 2048
CHUNK = 128
RWKV_CHUNK = 64
W_LORA = 64
A_LORA = 64
G_LORA = 256
RWKV_GN_EPS = 64e-5
MEM_HEAD_DIM = 128
CONV_WIDTH = 3
NORM_EPS = 1e-6
NEG_INF = -1e30
HALO = 16

F32 = jnp.float32
BF16 = jnp.bfloat16
VMEM_LIMIT_BYTES = 56 * 1024 * 1024


def _params(*sem):
    return pltpu.CompilerParams(dimension_semantics=sem, vmem_limit_bytes=VMEM_LIMIT_BYTES)


def _dot(a, b):
    return jnp.dot(a, b, preferred_element_type=F32)


def _dot_nt(a, b):
    return lax.dot_general(a, b, (((1,), (1,)), ((), ())), preferred_element_type=F32)


def _dot_tn(a, b):
    return lax.dot_general(a, b, (((0,), (0,)), ((), ())), preferred_element_type=F32)


def _run_interleaved(stage_generators):
    for _ in itertools.zip_longest(*stage_generators):
        pass


def _rms_rows(x, gain):
    ms = jnp.mean(x * x, axis=-1, keepdims=True)
    return x * lax.rsqrt(ms + NORM_EPS) * gain


def _norm_rows_with_halo(x_ref, xh_ref, g_ref, h_ref, at_seq_start, sub):
    halo = _rms_rows(xh_ref[...], g_ref[...])
    h_ref[0:HALO, :] = jnp.where(at_seq_start, 0.0, halo).astype(BF16)

    def body(t, c):
        rows = pl.ds(pl.multiple_of(t * sub, sub), sub)
        h_ref[pl.ds(pl.multiple_of(HALO + t * sub, HALO), sub), :] = (
            _rms_rows(x_ref[rows, :], g_ref[...]).astype(BF16))
        return c
    lax.fori_loop(0, x_ref.shape[0] // sub, body, 0)


def _gelu_tanh(x):
    return 0.5 * x * (1.0 + jnp.tanh(np.sqrt(2.0 / np.pi).astype(np.float32) * (x + 0.044715 * (x * x * x))))


def _sigmoid(x):
    return 1.0 / (1.0 + jnp.exp(-x))


def _rms_matmul_kernel(x_ref, g_ref, w_ref, o_ref, h_ref, *, sub):
    @pl.when(pl.program_id(1) == 0)
    def _():
        def body(i, c):
            rows = pl.ds(pl.multiple_of(i * sub, sub), sub)
            h_ref[rows, :] = _rms_rows(x_ref[rows, :], g_ref[...]).astype(BF16)
            return c
        lax.fori_loop(0, x_ref.shape[0] // sub, body, 0)

    o_ref[...] = _dot(h_ref[...], w_ref[...]).astype(o_ref.dtype)


def rms_matmul(x, gain, w, tm, tn, out_dtype=F32):
    m, k = x.shape
    n = w.shape[1]
    assert m % tm == 0 and n % tn == 0
    sub = min(tm, 256)
    return pl.pallas_call(
        functools.partial(_rms_matmul_kernel, sub=sub),
        grid=(m // tm, n // tn),
        in_specs=[pl.BlockSpec((tm, k), lambda i, j: (i, 0)),
                  pl.BlockSpec((1, k), lambda i, j: (0, 0)),
                  pl.BlockSpec((k, tn), lambda i, j: (0, j))],
        out_specs=pl.BlockSpec((tm, tn), lambda i, j: (i, j)),
        out_shape=jax.ShapeDtypeStruct((m, n), out_dtype),
        scratch_shapes=[pltpu.VMEM((tm, k), BF16)],
        compiler_params=_params("parallel", "arbitrary"),
        name="rms_matmul",
    )(x, gain.reshape(1, k), w)


def _in_proj_kernel(x_ref, xh_ref, g_ref, w_ref, mu_ref, o_ref, h_ref, *, blocks_per_seq, sub):
    @pl.when(pl.program_id(1) == 0)
    def _():
        _norm_rows_with_halo(x_ref, xh_ref, g_ref, h_ref, pl.program_id(0) % blocks_per_seq == 0, sub)

    z = _dot(h_ref[...], w_ref[...])
    shifted = z + (pltpu.roll(z, 1, axis=0) - z) * mu_ref[...]
    o_ref[...] = shifted[HALO:, :]


def in_projection(x, gain, w, mu, seq, tm, tn):
    m, k = x.shape
    n = w.shape[1]
    assert m % tm == 0 and n % tn == 0 and seq % tm == 0
    sub = min(tm, 256)
    return pl.pallas_call(
        functools.partial(_in_proj_kernel, blocks_per_seq=seq // tm, sub=sub),
        grid=(m // tm, n // tn),
        in_specs=[pl.BlockSpec((tm, k), lambda i, j: (i, 0)),
                  pl.BlockSpec((HALO, k), lambda i, j: (jnp.maximum(i * (tm // HALO) - 1, 0), 0)),
                  pl.BlockSpec((1, k), lambda i, j: (0, 0)),
                  pl.BlockSpec((k, tn), lambda i, j: (0, j)),
                  pl.BlockSpec((1, tn), lambda i, j: (0, j))],
        out_specs=pl.BlockSpec((tm, tn), lambda i, j: (i, j)),
        out_shape=jax.ShapeDtypeStruct((m, n), F32),
        scratch_shapes=[pltpu.VMEM((tm + HALO, k), BF16)],
        compiler_params=_params("parallel", "arbitrary"),
        name="in_projection",
    )(x, x, gain.reshape(1, k), w, mu.reshape(1, n))


def _t5_causal_bucket(dist):
    max_exact = REL_BUCKETS // 2
    d = np.maximum(dist, 0)
    scaled = np.log(np.maximum(d, 1) / max_exact) / np.log(REL_MAX_DISTANCE / max_exact)
    large = np.minimum(max_exact + (scaled * (REL_BUCKETS - max_exact)).astype(np.int32), REL_BUCKETS - 1)
    return np.where(d < max_exact, d, large).astype(np.int32)


def _bucket_tables():
    out = []
    qi = np.arange(BLOCK)[:, None]
    kj = np.arange(2 * BLOCK)[None, :]
    for window, dilation in DILATED_PATTERNS:
        steps = window // dilation
        assert steps <= BLOCK
        delta = qi + BLOCK - kj
        band = (delta >= 0) & (delta <= steps)
        bucket = _t5_causal_bucket(np.clip(delta, 0, steps) * dilation)
        out.append(np.where(band, bucket, -1))
    return np.stack(out).astype(np.int32)


def _dilated_attn_kernel(table_ref, bucket_ref, q_ref, k_ref, v_ref, o_ref, bias_ref, acc_ref, lse_ref,
                         q4_ref, k4_ref, v4_ref, *, seq, n_pairs, unroll):
    hp = pl.program_id(0)
    quarter = seq // DEINTERLEAVE
    blocks_per_residue = quarter // BLOCK

    def token_rows(i):
        r4 = i // blocks_per_residue
        blk = i % blocks_per_residue
        return pl.ds(r4 + blk * (DEINTERLEAVE * BLOCK), BLOCK, stride=DEINTERLEAVE)

    def deinterleave(i, carry):
        dst = pl.ds(pl.multiple_of(i * BLOCK, BLOCK), BLOCK)
        src = token_rows(i)
        q4_ref[dst, :] = q_ref[src, :]
        k4_ref[dst, :] = k_ref[src, :]
        v4_ref[dst, :] = v_ref[src, :]
        return carry

    lax.fori_loop(0, seq // BLOCK, deinterleave, 0)
    lane = lax.broadcasted_iota(jnp.int32, (BLOCK, LANES), 1)
    lo_half = lane < HEAD_DIM

    @pl.when(pl.program_id(1) == 0)
    def _():
        key = lax.broadcasted_iota(jnp.int32, (BLOCK, 2 * BLOCK), 1)
        for p in range(len(DILATED_PATTERNS)):
            bucket = bucket_ref[p]
            for h in range(2):
                val = jnp.zeros(bucket.shape, F32)
                for b in range(REL_BUCKETS):
                    val = jnp.where(bucket == b, table_ref[b, 2 * hp + h], val)
                val = jnp.where(bucket < 0, NEG_INF, val)
                bias_ref[p, h, 0] = val
                bias_ref[p, h, 1] = jnp.where(key < BLOCK, NEG_INF, val)

    scale = HEAD_DIM ** -0.5
    for p, (window, d) in enumerate(DILATED_PATTERNS):
        n_blk = seq // (d * BLOCK)
        shift = int(np.log2(d))

        def block(idx, p=p, d=d, shift=shift):
            r = idx & (d - 1)
            n = idx >> shift
            n_prev = jnp.maximum(n - 1, 0)
            if d == 1:
                qs, ks, vs = q_ref, k_ref, v_ref
                cur_rows = pl.ds(pl.multiple_of(n * BLOCK, BLOCK), BLOCK)
                prev_rows = pl.ds(pl.multiple_of(n_prev * BLOCK, BLOCK), BLOCK)
            elif d == DEINTERLEAVE:
                qs, ks, vs = q4_ref, k4_ref, v4_ref
                cur_rows = pl.ds(pl.multiple_of(r * quarter + n * BLOCK, BLOCK), BLOCK)
                prev_rows = pl.ds(pl.multiple_of(r * quarter + n_prev * BLOCK, BLOCK), BLOCK)
            else:
                qs, ks, vs = q4_ref, k4_ref, v4_ref
                sub = d // DEINTERLEAVE
                base = (r % DEINTERLEAVE) * quarter + r // DEINTERLEAVE
                cur_rows = pl.ds(base + n * (sub * BLOCK), BLOCK, stride=sub)
                prev_rows = pl.ds(base + n_prev * (sub * BLOCK), BLOCK, stride=sub)
            q = qs[cur_rows, :] * scale
            kcat = jnp.concatenate([ks[prev_rows, :], ks[cur_rows, :]], axis=0).astype(BF16)
            vcat = jnp.concatenate([vs[prev_rows, :], vs[cur_rows, :]], axis=0).astype(BF16)
            var = jnp.where(n == 0, 1, 0)
            scores = [_dot_nt(jnp.where(lo_half if h == 0 else ~lo_half, q, 0.0).astype(BF16), kcat)
                      for h in range(2)]
            yield
            outs, lses, dens = [], [], []
            for h in range(2):
                s = scores[h] + bias_ref[p, h, var]
                m = jnp.max(s, axis=-1, keepdims=True)
                e = jnp.exp(s - m)
                den = jnp.sum(e, axis=-1, keepdims=True)
                outs.append(_dot(e.astype(BF16), vcat))
                dens.append(den)
                lses.append(m + jnp.log(den))
            yield
            acc_ref[p, cur_rows, :] = jnp.where(lo_half, outs[0] * (1.0 / dens[0]), outs[1] * (1.0 / dens[1]))
            lse_ref[p, cur_rows, :] = jnp.where(lo_half, lses[0], lses[1])

        def block_group(i, carry, block=block):
            _run_interleaved([block(i * unroll + u) for u in range(unroll)])
            return carry

        lax.fori_loop(0, d * n_blk // unroll, block_group, 0)

    def merge(i, carry):
        rows = pl.ds(pl.multiple_of(i * BLOCK, BLOCK), BLOCK)
        tok = token_rows(i)
        l0, l1, l2 = lse_ref[0, tok, :], lse_ref[1, rows, :], lse_ref[2, rows, :]
        mx = jnp.maximum(jnp.maximum(l0, l1), l2)
        w0, w1, w2 = jnp.exp(l0 - mx), jnp.exp(l1 - mx), jnp.exp(l2 - mx)
        num = w0 * acc_ref[0, tok, :] + w1 * acc_ref[1, rows, :] + w2 * acc_ref[2, rows, :]
        o_ref[tok, :] = num / (w0 + w1 + w2)
        return carry

    lax.fori_loop(0, seq // BLOCK, merge, 0)


def dilated_attention(z, col0, rel_bias_table, a_heads):
    b, s, _ = z.shape
    n_pairs = a_heads // 2
    unroll = min(32, s // BLOCK)
    assert s % (DILATED_PATTERNS[-1][1] * BLOCK) == 0 and (s // BLOCK) % unroll == 0
    assert [d for _, d in DILATED_PATTERNS] == [1, DEINTERLEAVE, DEINTERLEAVE ** 2]
    buckets = jnp.asarray(_bucket_tables())
    n_pat = len(DILATED_PATTERNS)
    return pl.pallas_call(
        functools.partial(_dilated_attn_kernel, seq=s, n_pairs=n_pairs, unroll=unroll),
        grid=(n_pairs, b),
        in_specs=[pl.BlockSpec(memory_space=pltpu.SMEM),
                  pl.BlockSpec((n_pat, BLOCK, 2 * BLOCK), lambda hp, bi: (0, 0, 0)),
                  pl.BlockSpec((None, s, LANES), lambda hp, bi: (bi, 0, col0 + hp)),
                  pl.BlockSpec((None, s, LANES), lambda hp, bi: (bi, 0, col0 + n_pairs + hp)),
                  pl.BlockSpec((None, s, LANES), lambda hp, bi: (bi, 0, col0 + 2 * n_pairs + hp))],
        out_specs=pl.BlockSpec((None, s, LANES), lambda hp, bi: (bi, 0, hp)),
        out_shape=jax.ShapeDtypeStruct((b, s, a_heads * HEAD_DIM), F32),
        scratch_shapes=[pltpu.VMEM((n_pat, 2, 2, BLOCK, 2 * BLOCK), F32),
                        pltpu.VMEM((n_pat, s, LANES), F32),
                        pltpu.VMEM((n_pat, s, LANES), F32)] + [pltpu.VMEM((s, LANES), F32)] * 3,
        compiler_params=_params("parallel", "arbitrary"),
        name="dilated_attention",
    )(rel_bias_table, buckets, z, z, z)


def _sgu_kernel(z_ref, ng_ref, w_ref, b_ref, og_ref, o_ref, *, width, groups):
    z = _gelu_tanh(z_ref[...])
    u = z[:, :width]
    g = z[:, width:]
    gc = g - jnp.mean(g, axis=-1, keepdims=True)
    gn = gc * lax.rsqrt(jnp.mean(gc * gc, axis=-1, keepdims=True) + NORM_EPS) * ng_ref[...]
    ri = lax.broadcasted_iota(jnp.int32, (CHUNK, CHUNK), 0)
    ci = lax.broadcasted_iota(jnp.int32, (CHUNK, CHUNK), 1)
    causal = ri >= ci
    lane = lax.broadcasted_iota(jnp.int32, (CHUNK, LANES), 1)
    lo_half = lane < HEAD_DIM
    rows = z.shape[0]
    cols = []
    for j in range(groups // 2):
        w0 = jnp.where(causal, w_ref[2 * j], 0.0).astype(BF16)
        w1 = jnp.where(causal, w_ref[2 * j + 1], 0.0).astype(BF16)
        parts = []
        for c in range(rows // CHUNK):
            blk = gn[c * CHUNK:(c + 1) * CHUNK, j * LANES:(j + 1) * LANES]
            g0 = jnp.where(lo_half, blk, 0.0).astype(BF16)
            g1 = jnp.where(lo_half, 0.0, blk).astype(BF16)
            parts.append(_dot(w0, g0) + _dot(w1, g1) + b_ref[:, j * LANES:(j + 1) * LANES])
        cols.append(jnp.concatenate(parts, axis=0))
    mixed = jnp.concatenate(cols, axis=1)
    o_ref[...] = _rms_rows(u * mixed, og_ref[...]).astype(o_ref.dtype)


def spatial_gating(z, col_blk, norm_gain, w_s, b_s, out_gain, tm):
    m = z.shape[0]
    width = norm_gain.shape[0]
    two_w = 2 * width
    groups = w_s.shape[0]
    bias = jnp.repeat(b_s.T, HEAD_DIM, axis=1)
    return pl.pallas_call(
        functools.partial(_sgu_kernel, width=width, groups=groups),
        grid=(m // tm,),
        in_specs=[pl.BlockSpec((tm, two_w), lambda i: (i, col_blk)),
                  pl.BlockSpec((1, width), lambda i: (0, 0)),
                  pl.BlockSpec((groups, CHUNK, CHUNK), lambda i: (0, 0, 0)),
                  pl.BlockSpec((CHUNK, width), lambda i: (0, 0)),
                  pl.BlockSpec((1, width), lambda i: (0, 0))],
        out_specs=pl.BlockSpec((tm, width), lambda i: (i, 0)),
        out_shape=jax.ShapeDtypeStruct((m, width), BF16),
        compiler_params=_params("parallel"),
        name="spatial_gating",
    )(z, norm_gain.reshape(1, width), w_s, bias, out_gain.reshape(1, width))


def _head_sums(z, lo_half):
    s0 = jnp.sum(jnp.where(lo_half, z, 0.0), axis=-1, keepdims=True)
    s1 = jnp.sum(jnp.where(lo_half, 0.0, z), axis=-1, keepdims=True)
    return jnp.where(lo_half, s0, s1)


def _stack_heads(z, lo_half):
    return jnp.concatenate([jnp.where(lo_half, z, 0.0), jnp.where(lo_half, 0.0, z)], axis=0)


def _rwkv_chunk_kernel(r_ref, k_ref, v_ref, lo_ref, g0_ref, g1_ref,
                       w0_ref, wup_ref, a0_ref, aup_ref, gup_ref, kk_ref, ka_ref, rk_ref,
                       mc_ref, qp_ref, nc_ref, y1_ref, bonus_ref, gate_ref, e_ref, a_ref, *, ts, unroll):
    c = RWKV_CHUNK

    lo = lo_ref[...]
    z = w0_ref[...] + _dot(jnp.tanh(lo).astype(BF16), wup_ref[...])
    w_log = -(jnp.maximum(-z, 0.0) + jnp.log(1.0 + jnp.exp(-jnp.abs(z)))) - 0.5
    e_ref[...] = jnp.exp(w_log)
    a_ref[...] = _sigmoid(a0_ref[...] + _dot(lo.astype(BF16), aup_ref[...]))
    gl = jnp.concatenate([g0_ref[...], g1_ref[...]], axis=1)
    gate_ref[...] = _dot(_sigmoid(gl).astype(BF16), gup_ref[...])
    lane = lax.broadcasted_iota(jnp.int32, (c, LANES), 1)
    lo_half = lane < HEAD_DIM
    trow = lax.broadcasted_iota(jnp.int32, (c, LANES), 0)
    ri = lax.broadcasted_iota(jnp.int32, (2 * c, 2 * c), 0)
    ci = lax.broadcasted_iota(jnp.int32, (2 * c, 2 * c), 1)
    same = (ri // c) == (ci // c)
    strict = same & ((ri % c) > (ci % c))
    incl = same & ((ri % c) >= (ci % c))
    diag = ri == ci

    def chunk(j):
        rows = pl.ds(pl.multiple_of(j * c, c), c)
        r = r_ref[rows, :]
        kx = k_ref[rows, :]
        v = v_ref[rows, :]
        e = e_ref[rows, :]
        a = a_ref[rows, :]
        kk = kx * kk_ref[...]
        kk = kk / jnp.maximum(jnp.sqrt(_head_sums(kk * kk, lo_half)), 1e-12)
        k = kx * (1.0 + (a - 1.0) * ka_ref[...])
        bonus_ref[rows, :] = _head_sums(r * k * rk_ref[...], lo_half) * v

        cs = e
        for sh in (1, 2, 4, 8, 16, 32):
            cs = cs + jnp.where(trow >= sh, pltpu.roll(cs, sh, axis=0), 0.0)
        cs_end = cs[c - 1:c, :]
        p = jnp.exp(-cs)
        p_before = jnp.exp(e - cs)
        p_inv = jnp.exp(cs)
        p_rest = jnp.exp(cs - cs_end)
        be = kk * a
        at = _stack_heads(-kk * p_before, lo_half)
        rt = _stack_heads(r * p, lo_half)
        bt = _stack_heads(be * p_inv, lo_half).astype(BF16)
        kt = _stack_heads(k * p_inv, lo_half).astype(BF16)
        bh = _stack_heads(be * p_rest, lo_half).astype(BF16)
        kh = _stack_heads(k * p_rest, lo_half).astype(BF16)
        vs = _stack_heads(v, lo_half).astype(BF16)

        cc = _dot_nt(jnp.concatenate([at, rt], axis=0).astype(BF16), jnp.concatenate([bt, kt], axis=0))
        yield
        l_ab = jnp.where(strict, cc[:2 * c, :2 * c], 0.0)
        l_ak = jnp.where(strict, cc[:2 * c, 2 * c:], 0.0)
        m_rb = jnp.where(incl, cc[2 * c:, :2 * c], 0.0)
        m_rk = jnp.where(incl, cc[2 * c:, 2 * c:], 0.0)

        lv = _dot(jnp.concatenate([l_ak, m_rk], axis=0).astype(BF16), vs)
        x = jnp.concatenate([at, lv[:2 * c, :]], axis=1)
        yield
        lp = l_ab.astype(BF16)
        n = 1
        while n < c:
            x = x + _dot(lp, x.astype(BF16))
            yield
            n *= 2
            if n < c:
                lp = _dot(lp, lp).astype(BF16)
        x16 = x.astype(BF16)

        mw = _dot(m_rb.astype(BF16), x16)
        yield
        qp_ref[j] = (rt + mw[:, :LANES]).astype(BF16)
        y1_ref[j] = mw[:, LANES:] + lv[2 * c:, :]
        bw = _dot_tn(bh, x16)
        p_end = jnp.exp(-cs_end)
        decay_end = jnp.where(diag, jnp.broadcast_to(p_end, (2 * c, LANES)), 0.0)
        mc_ref[j] = (decay_end + bw[:, :LANES]).astype(BF16)
        nc_ref[j] = bw[:, LANES:] + _dot_tn(kh, vs)

    def chunk_group(i, carry):
        _run_interleaved([chunk(i * unroll + u) for u in range(unroll)])
        return carry

    lax.fori_loop(0, ts // (c * unroll), chunk_group, 0)


def rwkv_chunks(z, col0, w0, w_up, a0, a_up, g_up, k_k, k_a, r_k, c_heads, ts):
    b, s, _ = z.shape
    cw = c_heads * HEAD_DIM
    n_pairs = c_heads // 2
    c = RWKV_CHUNK
    n_chunks = s // c
    cpb = ts // c
    assert 2 * c == LANES and W_LORA + A_LORA == LANES and G_LORA == 2 * LANES
    col_lo = col0 + 3 * n_pairs
    zero = jnp.zeros((A_LORA, cw), F32)
    wup_ext = jnp.concatenate([w_up, zero], axis=0).astype(BF16)
    aup_ext = jnp.concatenate([zero, a_up], axis=0).astype(BF16)

    def tok(col):
        return pl.BlockSpec((None, ts, LANES), lambda bi, hp, t, col=col: (bi, t, col(hp)))

    cols = [lambda hp: col0 + hp, lambda hp: col0 + n_pairs + hp, lambda hp: col0 + 2 * n_pairs + hp,
            lambda hp: col_lo, lambda hp: col_lo + 1, lambda hp: col_lo + 2]

    def per_pair(rows):
        return pl.BlockSpec((rows, LANES), lambda bi, hp, t: (0, hp))

    mat = lambda dt: jax.ShapeDtypeStruct((b, n_pairs, n_chunks, LANES, LANES), dt)
    mat_spec = pl.BlockSpec((None, None, cpb, LANES, LANES), lambda bi, hp, t: (bi, hp, t, 0, 0))
    tok_out = pl.BlockSpec((None, ts, LANES), lambda bi, hp, t: (bi, t, hp))
    return pl.pallas_call(
        functools.partial(_rwkv_chunk_kernel, ts=ts, unroll=min(16, cpb)),
        grid=(b, n_pairs, s // ts),
        in_specs=[tok(cf) for cf in cols] + [
            per_pair(1), per_pair(LANES), per_pair(1), per_pair(LANES), per_pair(G_LORA),
            per_pair(1), per_pair(1), per_pair(1)],
        out_specs=[mat_spec, mat_spec, mat_spec, mat_spec, tok_out, tok_out],
        out_shape=[mat(BF16), mat(BF16), mat(F32), mat(F32),
                   jax.ShapeDtypeStruct((b, s, cw), F32), jax.ShapeDtypeStruct((b, s, cw), F32)],
        scratch_shapes=[pltpu.VMEM((ts, LANES), F32)] * 2,
        compiler_params=_params("parallel", "parallel", "parallel"),
        name="rwkv_chunks",
    )(z, z, z, z, z, z, w0.reshape(1, cw), wup_ext, a0.reshape(1, cw), aup_ext, g_up.astype(BF16),
      k_k.reshape(1, cw), k_a.reshape(1, cw), r_k.reshape(1, cw))


def _rwkv_state_kernel(mc_ref, qp_ref, nc_ref, y1_ref, bonus_ref, gate_ref, lg_ref, lb_ref, o_ref, h_ref,
                       *, n_pairs, cpb):
    c = RWKV_CHUNK

    @pl.when(pl.program_id(1) == 0)
    def _():
        h_ref[...] = jnp.zeros(h_ref.shape, F32)

    lane = lax.broadcasted_iota(jnp.int32, (c, LANES), 1)
    lo_half = lane < HEAD_DIM
    inv_n = 1.0 / HEAD_DIM

    def pair(g, j, rows):
        cols = slice(g * LANES, (g + 1) * LANES)
        both = _dot(jnp.concatenate([qp_ref[g, j], mc_ref[g, j]], axis=0), h_ref[g].astype(BF16))
        ys = both[:2 * c, :] + y1_ref[g, j]
        h_ref[g] = both[2 * c:, :] + nc_ref[g, j]
        yield
        y = ys[:c, :] + ys[c:, :]
        yc = y - _head_sums(y, lo_half) * inv_n
        yn = yc * lax.rsqrt(_head_sums(yc * yc, lo_half) * inv_n + RWKV_GN_EPS)
        yn = yn * lg_ref[:, cols] + lb_ref[:, cols]
        o_ref[rows, cols] = ((yn + bonus_ref[rows, cols]) * gate_ref[rows, cols]).astype(o_ref.dtype)

    def chunk(j, carry):
        rows = pl.ds(pl.multiple_of(j * c, c), c)
        _run_interleaved([pair(g, j, rows) for g in range(n_pairs)])
        return carry

    lax.fori_loop(0, cpb, chunk, 0)


def rwkv_state_scan(mc, qp, nc, y1, bonus, gate, ln_gain, ln_bias, cpb):
    b, n_pairs, n_chunks = mc.shape[:3]
    s = n_chunks * RWKV_CHUNK
    cw = n_pairs * LANES
    ts = cpb * RWKV_CHUNK
    mat_spec = pl.BlockSpec((None, n_pairs, cpb, LANES, LANES), lambda bi, t: (bi, 0, t, 0, 0))
    tok_spec = pl.BlockSpec((None, ts, cw), lambda bi, t: (bi, t, 0))
    vec_spec = pl.BlockSpec((1, cw), lambda bi, t: (0, 0))
    return pl.pallas_call(
        functools.partial(_rwkv_state_kernel, n_pairs=n_pairs, cpb=cpb),
        grid=(b, n_chunks // cpb),
        in_specs=[mat_spec, mat_spec, mat_spec, mat_spec, tok_spec, tok_spec, vec_spec, vec_spec],
        out_specs=tok_spec,
        out_shape=jax.ShapeDtypeStruct((b, s, cw), BF16),
        scratch_shapes=[pltpu.VMEM((n_pairs, LANES, LANES), F32)],
        compiler_params=_params("parallel", "arbitrary"),
        name="rwkv_state_scan",
    )(mc, qp, nc, y1, bonus, gate, ln_gain.reshape(1, cw), ln_bias.reshape(1, cw))


def _mixer_out_kernel(oa_ref, ob_ref, oc_ref, ag_ref, wa_ref, wb_ref, wc_ref, x_ref, g_ref, o_ref):
    oa = _rms_rows(oa_ref[...], ag_ref[...]).astype(BF16)
    y = _dot(oa, wa_ref[...]) + _dot(ob_ref[...], wb_ref[...]) + _dot(oc_ref[...], wc_ref[...])
    o_ref[...] = x_ref[...] + _rms_rows(y, g_ref[...])


def mixer_out(oa, ob, oc, attn_gain, w_out, x, gain, tm):
    m, d = x.shape
    aw, bw, cw = oa.shape[1], ob.shape[1], oc.shape[1]
    w = w_out.astype(BF16)
    row = lambda width: pl.BlockSpec((tm, width), lambda i: (i, 0))
    full = lambda r, cdim: pl.BlockSpec((r, cdim), lambda i: (0, 0))
    return pl.pallas_call(
        _mixer_out_kernel,
        grid=(m // tm,),
        in_specs=[row(aw), row(bw), row(cw), full(1, aw), full(aw, d), full(bw, d), full(cw, d),
                  row(d), full(1, d)],
        out_specs=row(d),
        out_shape=jax.ShapeDtypeStruct((m, d), F32),
        compiler_params=_params("parallel"),
        name="mixer_out",
    )(oa, ob, oc, attn_gain.reshape(1, aw), w[:aw], w[aw:aw + bw], w[aw + bw:], x, gain.reshape(1, d))


def _cross_attn_kernel(x_ref, gi_ref, wq_ref, kv_ref, wo_ref, go_ref, gn_ref, o_ref, hn_ref, *, heads):
    x = x_ref[...]
    q = _dot(_rms_rows(x, gi_ref[...]).astype(BF16), wq_ref[...])
    inner = heads * MEM_HEAD_DIM
    scale = MEM_HEAD_DIM ** -0.5
    outs = [None] * heads

    def head(h):
        cols = slice(h * MEM_HEAD_DIM, (h + 1) * MEM_HEAD_DIM)
        s = _dot_nt(q[:, cols].astype(BF16), kv_ref[:, cols]) * scale
        yield
        m = jnp.max(s, axis=-1, keepdims=True)
        e = jnp.exp(s - m)
        p = e * (1.0 / jnp.sum(e, axis=-1, keepdims=True))
        outs[h] = _dot(p.astype(BF16), kv_ref[:, inner + h * MEM_HEAD_DIM:inner + (h + 1) * MEM_HEAD_DIM])

    _run_interleaved([head(h) for h in range(heads)])
    o = jnp.concatenate(outs, axis=1).astype(BF16)
    x_out = x + _rms_rows(_dot(o, wo_ref[...]), go_ref[...])
    o_ref[...] = x_out
    hn_ref[...] = _rms_rows(x_out, gn_ref[...]).astype(hn_ref.dtype)


def cross_attention(x, gain_in, wq, kv, wo, gain_out, gain_next, seq, tm):
    m, d = x.shape
    inner = wq.shape[1]
    heads = inner // MEM_HEAD_DIM
    mem_tokens = kv.shape[0] // (m // seq)
    blocks_per_seq = seq // tm
    full = lambda r, cdim: pl.BlockSpec((r, cdim), lambda i: (0, 0))
    return pl.pallas_call(
        functools.partial(_cross_attn_kernel, heads=heads),
        grid=(m // tm,),
        in_specs=[pl.BlockSpec((tm, d), lambda i: (i, 0)), full(1, d), full(d, inner),
                  pl.BlockSpec((mem_tokens, 2 * inner), lambda i: (i // blocks_per_seq, 0)),
                  full(inner, d), full(1, d), full(1, d)],
        out_specs=[pl.BlockSpec((tm, d), lambda i: (i, 0)), pl.BlockSpec((tm, d), lambda i: (i, 0))],
        out_shape=[jax.ShapeDtypeStruct((m, d), F32), jax.ShapeDtypeStruct((m, d), BF16)],
        compiler_params=_params("parallel"),
        name="cross_attention",
    )(x, gain_in.reshape(1, d), wq.astype(BF16), kv, wo.astype(BF16), gain_out.reshape(1, d),
      gain_next.reshape(1, d))


def _conv_ffn_kernel(h_ref, hh_ref, wg_ref, wv_ref, cwg_ref, cwv_ref, cbg_ref, cbv_ref, wd_ref, x_ref, go_ref,
                     o_ref, hs_ref, acc_ref, inv_ref, *, blocks_per_seq, nf, n_col):
    i = pl.program_id(0)
    f = pl.program_id(1)
    tc = o_ref.shape[1]

    @pl.when(f == 0)
    def _():
        hs_ref[0:HALO, :] = jnp.where(i % blocks_per_seq == 0, jnp.zeros_like(hh_ref[...]), hh_ref[...])
        hs_ref[HALO:, :] = h_ref[...]
        acc_ref[...] = jnp.zeros(acc_ref.shape, F32)

    @pl.when(f < nf)
    def _():
        def conv(w_ref, cw_ref, cb_ref):
            up = _dot(hs_ref[...], w_ref[...])
            return (cb_ref[...] + pltpu.roll(up, 2, axis=0) * cw_ref[0:1, :]
                    + pltpu.roll(up, 1, axis=0) * cw_ref[1:2, :] + up * cw_ref[2:3, :])

        gate = conv(wg_ref, cwg_ref, cbg_ref)
        val = conv(wv_ref, cwv_ref, cbv_ref)
        act = (_gelu_tanh(gate) * val)[HALO:, :].astype(BF16)
        for c in range(n_col):
            acc_ref[c] += _dot(act, wd_ref[:, c * tc:(c + 1) * tc])

    @pl.when(f == nf)
    def _():
        ss = jnp.zeros((acc_ref.shape[1], 1), F32)
        for c in range(n_col):
            y = acc_ref[c]
            ss = ss + jnp.sum(y * y, axis=-1, keepdims=True)
        inv_ref[...] = lax.rsqrt(ss * (1.0 / (n_col * tc)) + NORM_EPS)

    @pl.when(f >= nf)
    def _():
        o_ref[...] = x_ref[...] + acc_ref[f - nf] * inv_ref[...] * go_ref[...]


def conv_ffn(h, x, conv_w, conv_b, w_up, w_down, gain_out, seq, tm, tf, tc):
    m, d = x.shape
    d_ff = w_down.shape[0]
    nf = d_ff // tf
    n_col = d // tc
    assert d_ff % tf == 0 and d % tc == 0 and seq % tm == 0
    assert conv_w.shape[0] == CONV_WIDTH and CONV_WIDTH - 1 <= HALO
    wu = w_up.astype(BF16)
    cb = conv_b.reshape(1, 2 * d_ff)
    up_blk = lambda f: jnp.minimum(f, nf - 1)
    col_blk = lambda f: jnp.maximum(f - nf, 0)
    return pl.pallas_call(
        functools.partial(_conv_ffn_kernel, blocks_per_seq=seq // tm, nf=nf, n_col=n_col),
        grid=(m // tm, nf + n_col),
        in_specs=[pl.BlockSpec((tm, d), lambda i, f: (i, 0)),
                  pl.BlockSpec((HALO, d), lambda i, f: (jnp.maximum(i * (tm // HALO) - 1, 0), 0)),
                  pl.BlockSpec((d, tf), lambda i, f: (0, up_blk(f))),
                  pl.BlockSpec((d, tf), lambda i, f: (0, nf + up_blk(f))),
                  pl.BlockSpec((CONV_WIDTH, tf), lambda i, f: (0, up_blk(f))),
                  pl.BlockSpec((CONV_WIDTH, tf), lambda i, f: (0, nf + up_blk(f))),
                  pl.BlockSpec((1, tf), lambda i, f: (0, up_blk(f))),
                  pl.BlockSpec((1, tf), lambda i, f: (0, nf + up_blk(f))),
                  pl.BlockSpec((tf, d), lambda i, f: (up_blk(f), 0)),
                  pl.BlockSpec((tm, tc), lambda i, f: (i, col_blk(f))),
                  pl.BlockSpec((1, tc), lambda i, f: (0, col_blk(f)))],
        out_specs=pl.BlockSpec((tm, tc), lambda i, f: (i, col_blk(f))),
        out_shape=jax.ShapeDtypeStruct((m, d), F32),
        scratch_shapes=[pltpu.VMEM((tm + HALO, d), BF16), pltpu.VMEM((n_col, tm, tc), F32),
                        pltpu.VMEM((tm, 1), F32)],
        compiler_params=_params("parallel", "arbitrary"),
        name="conv_ffn",
    )(h, h, wu, wu, conv_w, conv_w, cb, cb, w_down.astype(BF16), x, gain_out.reshape(1, d))


def _tile_sizes(seq):
    tm = 512 if seq % 512 == 0 else seq
    return dict(tm_tall=min(1024, seq), tn_proj=1024, tm=tm, tf=512, tc=512, ts=min(1024, seq), cpb=min(16, seq // RWKV_CHUNK))


def kernel(x, mem, rel_bias_table, sandwich_gains, mem_src_gain, w_in, w_out, attn_out_gain, sgu_norm_gain, sgu_w, sgu_b, sgu_out_gain, rwkv_mu, rwkv_w0, rwkv_w_up, rwkv_a0, rwkv_a_up, rwkv_g_up, rwkv_k_k, rwkv_k_a, rwkv_r_k, rwkv_ln_gain, rwkv_ln_bias, mem_wq, mem_wkv, mem_wo, ffn_w_up, ffn_conv_w, ffn_conv_b, ffn_w_down):
    b, s, d = x.shape
    depth = w_in.shape[0]
    a_heads = rel_bias_table.shape[1]
    aw = a_heads * HEAD_DIM
    bw = sgu_norm_gain.shape[1]
    cw = rwkv_w0.shape[1]
    c_heads = cw // HEAD_DIM
    t = _tile_sizes(s)
    m = b * s
    xf = x.reshape(m, d)
    memf = mem.reshape(b * mem.shape[1], d)
    for l in range(depth):
        g = sandwich_gains[l]
        wi = w_in[l]
        n_in = wi.shape[1]
        n_pad = -n_in % t["tn_proj"]
        wi = jnp.concatenate([wi[:, 3 * aw:3 * aw + 2 * bw], wi[:, :3 * aw], wi[:, 3 * aw + 2 * bw:],
                              jnp.zeros((d, n_pad), wi.dtype)], axis=1).astype(BF16)
        col_a = 2 * bw // LANES
        col_c = col_a + 3 * aw // LANES
        mu = jnp.concatenate([jnp.zeros((2 * bw + 3 * aw,), F32), rwkv_mu[l], jnp.zeros((n_pad,), F32)])
        z = in_projection(xf, g[0], wi, mu, s, t["tm_tall"], t["tn_proj"])
        z3 = z.reshape(b, s, n_in + n_pad)
        oa = dilated_attention(z3, col_a, rel_bias_table, a_heads)
        ob = spatial_gating(z, 0, sgu_norm_gain[l], sgu_w[l], sgu_b[l], sgu_out_gain[l], t["tm"])
        mc, qp, nc, y1, bonus, gate = rwkv_chunks(
            z3, col_c, rwkv_w0[l], rwkv_w_up[l], rwkv_a0[l], rwkv_a_up[l], rwkv_g_up[l],
            rwkv_k_k[l], rwkv_k_a[l], rwkv_r_k[l], c_heads, t["ts"])
        oc = rwkv_state_scan(mc, qp, nc, y1, bonus, gate, rwkv_ln_gain[l], rwkv_ln_bias[l], t["cpb"])
        xf = mixer_out(oa.reshape(m, aw), ob, oc.reshape(m, cw), attn_out_gain[l], w_out[l], xf, g[1], t["tm"])
        kv = rms_matmul(memf, mem_src_gain[l], mem_wkv[l].astype(BF16), memf.shape[0], mem_wkv.shape[2] // 2,
                        out_dtype=BF16)
        xf, hf = cross_attention(xf, g[2], mem_wq[l], kv, mem_wo[l], g[3], g[4], s, t["tm"])
        xf = conv_ffn(hf, xf, ffn_conv_w[l], ffn_conv_b[l], ffn_w_up[l], ffn_w_down[l], g[5], s,
                      t["tm_tall"], t["tf"], t["tc"])
    return xf.reshape(b, s, d)
```

```python
import functools
import itertools

import numpy as np
import jax
import jax.numpy as jnp
from jax import lax
from jax.experimental import pallas as pl
from jax.experimental.pallas import tpu as pltpu

HEAD_DIM = 64
LANES = 128
DILATED_PATTERNS = ((128, 1), (512, 4), (2048, 16))
BLOCK = 128
DEINTERLEAVE = 4
REL_BUCKETS = 32
REL_MAX_DISTANCE = 2048
CHUNK = 128
RWKV_CHUNK = 64
W_LORA = 64
A_LORA = 64
G_LORA = 256
RWKV_GN_EPS = 64e-5
MEM_HEAD_DIM = 128
CONV_WIDTH = 3
NORM_EPS = 1e-6
NEG_INF = -1e30
HALO = 16

F32 = jnp.float32
BF16 = jnp.bfloat16
VMEM_LIMIT_BYTES = 56 * 1024 * 1024


def _params(*sem):
    return pltpu.CompilerParams(dimension_semantics=sem, vmem_limit_bytes=VMEM_LIMIT_BYTES)


def _dot(a, b):
    return jnp.dot(a, b, preferred_element_type=F32)


def _dot_nt(a, b):
    return lax.dot_general(a, b, (((1,), (1,)), ((), ())), preferred_element_type=F32)


def _dot_tn(a, b):
    return lax.dot_general(a, b, (((0,), (0,)), ((), ())), preferred_element_type=F32)


def _run_interleaved(stage_generators):
    for _ in itertools.zip_longest(*stage_generators):
        pass


def _rms_rows(x, gain):
    ms = jnp.mean(x * x, axis=-1, keepdims=True)
    return x * lax.rsqrt(ms + NORM_EPS) * gain


def _norm_rows_with_halo(x_ref, xh_ref, g_ref, h_ref, at_seq_start, sub):
    halo = _rms_rows(xh_ref[...], g_ref[...])
    h_ref[0:HALO, :] = jnp.where(at_seq_start, 0.0, halo).astype(BF16)

    def body(t, c):
        rows = pl.ds(pl.multiple_of(t * sub, sub), sub)
        h_ref[pl.ds(pl.multiple_of(HALO + t * sub, HALO), sub), :] = (
            _rms_rows(x_ref[rows, :], g_ref[...]).astype(BF16))
        return c
    lax.fori_loop(0, x_ref.shape[0] // sub, body, 0)


def _gelu_tanh(x):
    return 0.5 * x * (1.0 + jnp.tanh(np.sqrt(2.0 / np.pi).astype(np.float32) * (x + 0.044715 * (x * x * x))))


def _sigmoid(x):
    return 1.0 / (1.0 + jnp.exp(-x))


def _rms_matmul_kernel(x_ref, g_ref, w_ref, o_ref, h_ref, *, sub):
    @pl.when(pl.program_id(1) == 0)
    def _():
        def body(i, c):
            rows = pl.ds(pl.multiple_of(i * sub, sub), sub)
            h_ref[rows, :] = _rms_rows(x_ref[rows, :], g_ref[...]).astype(BF16)
            return c
        lax.fori_loop(0, x_ref.shape[0] // sub, body, 0)

    o_ref[...] = _dot(h_ref[...], w_ref[...]).astype(o_ref.dtype)


def rms_matmul(x, gain, w, tm, tn, out_dtype=F32):
    m, k = x.shape
    n = w.shape[1]
    assert m % tm == 0 and n % tn == 0
    sub = min(tm, 256)
    return pl.pallas_call(
        functools.partial(_rms_matmul_kernel, sub=sub),
        grid=(m // tm, n // tn),
        in_specs=[pl.BlockSpec((tm, k), lambda i, j: (i, 0)),
                  pl.BlockSpec((1, k), lambda i, j: (0, 0)),
                  pl.BlockSpec((k, tn), lambda i, j: (0, j))],
        out_specs=pl.BlockSpec((tm, tn), lambda i, j: (i, j)),
        out_shape=jax.ShapeDtypeStruct((m, n), out_dtype),
        scratch_shapes=[pltpu.VMEM((tm, k), BF16)],
        compiler_params=_params("parallel", "arbitrary"),
        name="rms_matmul",
    )(x, gain.reshape(1, k), w)


def _in_proj_kernel(x_ref, xh_ref, g_ref, w_ref, mu_ref, o_ref, h_ref, *, blocks_per_seq, sub):
    @pl.when(pl.program_id(1) == 0)
    def _():
        _norm_rows_with_halo(x_ref, xh_ref, g_ref, h_ref, pl.program_id(0) % blocks_per_seq == 0, sub)

    z = _dot(h_ref[...], w_ref[...])
    shifted = z + (pltpu.roll(z, 1, axis=0) - z) * mu_ref[...]
    o_ref[...] = shifted[HALO:, :]


def in_projection(x, gain, w, mu, seq, tm, tn):
    m, k = x.shape
    n = w.shape[1]
    assert m % tm == 0 and n % tn == 0 and seq % tm == 0
    sub = min(tm, 256)
    return pl.pallas_call(
        functools.partial(_in_proj_kernel, blocks_per_seq=seq // tm, sub=sub),
        grid=(m // tm, n // tn),
        in_specs=[pl.BlockSpec((tm, k), lambda i, j: (i, 0)),
                  pl.BlockSpec((HALO, k), lambda i, j: (jnp.maximum(i * (tm // HALO) - 1, 0), 0)),
                  pl.BlockSpec((1, k), lambda i, j: (0, 0)),
                  pl.BlockSpec((k, tn), lambda i, j: (0, j)),
                  pl.BlockSpec((1, tn), lambda i, j: (0, j))],
        out_specs=pl.BlockSpec((tm, tn), lambda i, j: (i, j)),
        out_shape=jax.ShapeDtypeStruct((m, n), F32),
        scratch_shapes=[pltpu.VMEM((tm + HALO, k), BF16)],
        compiler_params=_params("parallel", "arbitrary"),
        name="in_projection",
    )(x, x, gain.reshape(1, k), w, mu.reshape(1, n))


def _t5_causal_bucket(dist):
    max_exact = REL_BUCKETS // 2
    d = np.maximum(dist, 0)
    scaled = np.log(np.maximum(d, 1) / max_exact) / np.log(REL_MAX_DISTANCE / max_exact)
    large = np.minimum(max_exact + (scaled * (REL_BUCKETS - max_exact)).astype(np.int32), REL_BUCKETS - 1)
    return np.where(d < max_exact, d, large).astype(np.int32)


def _bucket_tables():
    out = []
    qi = np.arange(BLOCK)[:, None]
    kj = np.arange(2 * BLOCK)[None, :]
    for window, dilation in DILATED_PATTERNS:
        steps = window // dilation
        assert steps <= BLOCK
        delta = qi + BLOCK - kj
        band = (delta >= 0) & (delta <= steps)
        bucket = _t5_causal_bucket(np.clip(delta, 0, steps) * dilation)
        out.append(np.where(band, bucket, -1))
    return np.stack(out).astype(np.int32)


def _dilated_attn_kernel(table_ref, bucket_ref, q_ref, k_ref, v_ref, o_ref, bias_ref, acc_ref, lse_ref,
                         q4_ref, k4_ref, v4_ref, *, seq, n_pairs, unroll):
    hp = pl.program_id(0)
    quarter = seq // DEINTERLEAVE
    blocks_per_residue = quarter // BLOCK

    def token_rows(i):
        r4 = i // blocks_per_residue
        blk = i % blocks_per_residue
        return pl.ds(r4 + blk * (DEINTERLEAVE * BLOCK), BLOCK, stride=DEINTERLEAVE)

    def deinterleave(i, carry):
        dst = pl.ds(pl.multiple_of(i * BLOCK, BLOCK), BLOCK)
        src = token_rows(i)
        q4_ref[dst, :] = q_ref[src, :]
        k4_ref[dst, :] = k_ref[src, :]
        v4_ref[dst, :] = v_ref[src, :]
        return carry

    lax.fori_loop(0, seq // BLOCK, deinterleave, 0)
    lane = lax.broadcasted_iota(jnp.int32, (BLOCK, LANES), 1)
    lo_half = lane < HEAD_DIM

    @pl.when(pl.program_id(1) == 0)
    def _():
        key = lax.broadcasted_iota(jnp.int32, (BLOCK, 2 * BLOCK), 1)
        for p in range(len(DILATED_PATTERNS)):
            bucket = bucket_ref[p]
            for h in range(2):
                val = jnp.zeros(bucket.shape, F32)
                for b in range(REL_BUCKETS):
                    val = jnp.where(bucket == b, table_ref[b, 2 * hp + h], val)
                val = jnp.where(bucket < 0, NEG_INF, val)
                bias_ref[p, h, 0] = val
                bias_ref[p, h, 1] = jnp.where(key < BLOCK, NEG_INF, val)

    scale = HEAD_DIM ** -0.5
    for p, (window, d) in enumerate(DILATED_PATTERNS):
        n_blk = seq // (d * BLOCK)
        shift = int(np.log2(d))

        def block(idx, p=p, d=d, shift=shift):
            r = idx & (d - 1)
            n = idx >> shift
            n_prev = jnp.maximum(n - 1, 0)
            if d == 1:
                qs, ks, vs = q_ref, k_ref, v_ref
                cur_rows = pl.ds(pl.multiple_of(n * BLOCK, BLOCK), BLOCK)
                prev_rows = pl.ds(pl.multiple_of(n_prev * BLOCK, BLOCK), BLOCK)
            elif d == DEINTERLEAVE:
                qs, ks, vs = q4_ref, k4_ref, v4_ref
                cur_rows = pl.ds(pl.multiple_of(r * quarter + n * BLOCK, BLOCK), BLOCK)
                prev_rows = pl.ds(pl.multiple_of(r * quarter + n_prev * BLOCK, BLOCK), BLOCK)
            else:
                qs, ks, vs = q4_ref, k4_ref, v4_ref
                sub = d // DEINTERLEAVE
                base = (r % DEINTERLEAVE) * quarter + r // DEINTERLEAVE
                cur_rows = pl.ds(base + n * (sub * BLOCK), BLOCK, stride=sub)
                prev_rows = pl.ds(base + n_prev * (sub * BLOCK), BLOCK, stride=sub)
            q = qs[cur_rows, :] * scale
            kcat = jnp.concatenate([ks[prev_rows, :], ks[cur_rows, :]], axis=0).astype(BF16)
            vcat = jnp.concatenate([vs[prev_rows, :], vs[cur_rows, :]], axis=0).astype(BF16)
            var = jnp.where(n == 0, 1, 0)
            scores = [_dot_nt(jnp.where(lo_half if h == 0 else ~lo_half, q, 0.0).astype(BF16), kcat)
                      for h in range(2)]
            yield
            outs, lses, dens = [], [], []
            for h in range(2):
                s = scores[h] + bias_ref[p, h, var]
                m = jnp.max(s, axis=-1, keepdims=True)
                e = jnp.exp(s - m)
                den = jnp.sum(e, axis=-1, keepdims=True)
                outs.append(_dot(e.astype(BF16), vcat))
                dens.append(den)
                lses.append(m + jnp.log(den))
            yield
            acc_ref[p, cur_rows, :] = jnp.where(lo_half, outs[0] * (1.0 / dens[0]), outs[1] * (1.0 / dens[1]))
            lse_ref[p, cur_rows, :] = jnp.where(lo_half, lses[0], lses[1])

        def block_group(i, carry, block=block):
            _run_interleaved([block(i * unroll + u) for u in range(unroll)])
            return carry

        lax.fori_loop(0, d * n_blk // unroll, block_group, 0)

    def merge(i, carry):
        rows = pl.ds(pl.multiple_of(i * BLOCK, BLOCK), BLOCK)
        tok = token_rows(i)
        l0, l1, l2 = lse_ref[0, tok, :], lse_ref[1, rows, :], lse_ref[2, rows, :]
        mx = jnp.maximum(jnp.maximum(l0, l1), l2)
        w0, w1, w2 = jnp.exp(l0 - mx), jnp.exp(l1 - mx), jnp.exp(l2 - mx)
        num = w0 * acc_ref[0, tok, :] + w1 * acc_ref[1, rows, :] + w2 * acc_ref[2, rows, :]
        o_ref[tok, :] = num / (w0 + w1 + w2)
        return carry

    lax.fori_loop(0, seq // BLOCK, merge, 0)


def dilated_attention(z, col0, rel_bias_table, a_heads):
    b, s, _ = z.shape
    n_pairs = a_heads // 2
    unroll = min(32, s // BLOCK)
    assert s % (DILATED_PATTERNS[-1][1] * BLOCK) == 0 and (s // BLOCK) % unroll == 0
    assert [d for _, d in DILATED_PATTERNS] == [1, DEINTERLEAVE, DEINTERLEAVE ** 2]
    buckets = jnp.asarray(_bucket_tables())
    n_pat = len(DILATED_PATTERNS)
    return pl.pallas_call(
        functools.partial(_dilated_attn_kernel, seq=s, n_pairs=n_pairs, unroll=unroll),
        grid=(n_pairs, b),
        in_specs=[pl.BlockSpec(memory_space=pltpu.SMEM),
                  pl.BlockSpec((n_pat, BLOCK, 2 * BLOCK), lambda hp, bi: (0, 0, 0)),
                  pl.BlockSpec((None, s, LANES), lambda hp, bi: (bi, 0, col0 + hp)),
                  pl.BlockSpec((None, s, LANES), lambda hp, bi: (bi, 0, col0 + n_pairs + hp)),
                  pl.BlockSpec((None, s, LANES), lambda hp, bi: (bi, 0, col0 + 2 * n_pairs + hp))],
        out_specs=pl.BlockSpec((None, s, LANES), lambda hp, bi: (bi, 0, hp)),
        out_shape=jax.ShapeDtypeStruct((b, s, a_heads * HEAD_DIM), F32),
        scratch_shapes=[pltpu.VMEM((n_pat, 2, 2, BLOCK, 2 * BLOCK), F32),
                        pltpu.VMEM((n_pat, s, LANES), F32),
                        pltpu.VMEM((n_pat, s, LANES), F32)] + [pltpu.VMEM((s, LANES), F32)] * 3,
        compiler_params=_params("parallel", "arbitrary"),
        name="dilated_attention",
    )(rel_bias_table, buckets, z, z, z)


def _sgu_kernel(z_ref, ng_ref, w_ref, b_ref, og_ref, o_ref, *, width, groups):
    z = _gelu_tanh(z_ref[...])
    u = z[:, :width]
    g = z[:, width:]
    gc = g - jnp.mean(g, axis=-1, keepdims=True)
    gn = gc * lax.rsqrt(jnp.mean(gc * gc, axis=-1, keepdims=True) + NORM_EPS) * ng_ref[...]
    ri = lax.broadcasted_iota(jnp.int32, (CHUNK, CHUNK), 0)
    ci = lax.broadcasted_iota(jnp.int32, (CHUNK, CHUNK), 1)
    causal = ri >= ci
    lane = lax.broadcasted_iota(jnp.int32, (CHUNK, LANES), 1)
    lo_half = lane < HEAD_DIM
    rows = z.shape[0]
    cols = []
    for j in range(groups // 2):
        w0 = jnp.where(causal, w_ref[2 * j], 0.0).astype(BF16)
        w1 = jnp.where(causal, w_ref[2 * j + 1], 0.0).astype(BF16)
        parts = []
        for c in range(rows // CHUNK):
            blk = gn[c * CHUNK:(c + 1) * CHUNK, j * LANES:(j + 1) * LANES]
            g0 = jnp.where(lo_half, blk, 0.0).astype(BF16)
            g1 = jnp.where(lo_half, 0.0, blk).astype(BF16)
            parts.append(_dot(w0, g0) + _dot(w1, g1) + b_ref[:, j * LANES:(j + 1) * LANES])
        cols.append(jnp.concatenate(parts, axis=0))
    mixed = jnp.concatenate(cols, axis=1)
    o_ref[...] = _rms_rows(u * mixed, og_ref[...]).astype(o_ref.dtype)


def spatial_gating(z, col_blk, norm_gain, w_s, b_s, out_gain, tm):
    m = z.shape[0]
    width = norm_gain.shape[0]
    two_w = 2 * width
    groups = w_s.shape[0]
    bias = jnp.repeat(b_s.T, HEAD_DIM, axis=1)
    return pl.pallas_call(
        functools.partial(_sgu_kernel, width=width, groups=groups),
        grid=(m // tm,),
        in_specs=[pl.BlockSpec((tm, two_w), lambda i: (i, col_blk)),
                  pl.BlockSpec((1, width), lambda i: (0, 0)),
                  pl.BlockSpec((groups, CHUNK, CHUNK), lambda i: (0, 0, 0)),
                  pl.BlockSpec((CHUNK, width), lambda i: (0, 0)),
                  pl.BlockSpec((1, width), lambda i: (0, 0))],
        out_specs=pl.BlockSpec((tm, width), lambda i: (i, 0)),
        out_shape=jax.ShapeDtypeStruct((m, width), BF16),
        compiler_params=_params("parallel"),
        name="spatial_gating",
    )(z, norm_gain.reshape(1, width), w_s, bias, out_gain.reshape(1, width))


def _head_sums(z, lo_half):
    s0 = jnp.sum(jnp.where(lo_half, z, 0.0), axis=-1, keepdims=True)
    s1 = jnp.sum(jnp.where(lo_half, 0.0, z), axis=-1, keepdims=True)
    return jnp.where(lo_half, s0, s1)


def _stack_heads(z, lo_half):
    return jnp.concatenate([jnp.where(lo_half, z, 0.0), jnp.where(lo_half, 0.0, z)], axis=0)


def _rwkv_chunk_kernel(r_ref, k_ref, v_ref, lo_ref, g0_ref, g1_ref,
                       w0_ref, wup_ref, a0_ref, aup_ref, gup_ref, kk_ref, ka_ref, rk_ref,
                       mc_ref, qp_ref, nc_ref, y1_ref, bonus_ref, gate_ref, e_ref, a_ref, *, ts, unroll):
    c = RWKV_CHUNK

    lo = lo_ref[...]
    z = w0_ref[...] + _dot(jnp.tanh(lo).astype(BF16), wup_ref[...])
    w_log = -(jnp.maximum(-z, 0.0) + jnp.log(1.0 + jnp.exp(-jnp.abs(z)))) - 0.5
    e_ref[...] = jnp.exp(w_log)
    a_ref[...] = _sigmoid(a0_ref[...] + _dot(lo.astype(BF16), aup_ref[...]))
    gl = jnp.concatenate([g0_ref[...], g1_ref[...]], axis=1)
    gate_ref[...] = _dot(_sigmoid(gl).astype(BF16), gup_ref[...])
    lane = lax.broadcasted_iota(jnp.int32, (c, LANES), 1)
    lo_half = lane < HEAD_DIM
    trow = lax.broadcasted_iota(jnp.int32, (c, LANES), 0)
    ri = lax.broadcasted_iota(jnp.int32, (2 * c, 2 * c), 0)
    ci = lax.broadcasted_iota(jnp.int32, (2 * c, 2 * c), 1)
    same = (ri // c) == (ci // c)
    strict = same & ((ri % c) > (ci % c))
    incl = same & ((ri % c) >= (ci % c))
    diag = ri == ci
    lane2 = lax.broadcasted_iota(jnp.int32, (2 * c, LANES), 1)
    row2 = lax.broadcasted_iota(jnp.int32, (2 * c, LANES), 0)
    lo_lanes = lane2 < HEAD_DIM
    top_rows = row2 < c
    own_half = lo_lanes == top_rows

    def fold(z):
        return z + pltpu.roll(z, HEAD_DIM, axis=1)

    def unfold(packed):
        swapped = pltpu.roll(packed, HEAD_DIM, axis=1)
        first = jnp.where(own_half, jnp.where(top_rows, packed, swapped), 0.0)
        second = jnp.where(own_half, jnp.where(top_rows, swapped, packed), 0.0)
        return first, second

    def chunk(j):
        rows = pl.ds(pl.multiple_of(j * c, c), c)
        r = r_ref[rows, :]
        kx = k_ref[rows, :]
        v = v_ref[rows, :]
        e = e_ref[rows, :]
        a = a_ref[rows, :]
        kk = kx * kk_ref[...]
        kk = kk / jnp.maximum(jnp.sqrt(_head_sums(kk * kk, lo_half)), 1e-12)
        k = kx * (1.0 + (a - 1.0) * ka_ref[...])
        bonus_ref[rows, :] = _head_sums(r * k * rk_ref[...], lo_half) * v

        cs = e
        for sh in (1, 2, 4, 8, 16, 32):
            cs = cs + jnp.where(trow >= sh, pltpu.roll(cs, sh, axis=0), 0.0)
        cs_end = cs[c - 1:c, :]
        p = jnp.exp(-cs)
        p_before = jnp.exp(e - cs)
        p_inv = jnp.exp(cs)
        p_rest = jnp.exp(cs - cs_end)
        be = kk * a
        at = _stack_heads(-kk * p_before, lo_half)
        rt = _stack_heads(r * p, lo_half)
        bt = _stack_heads(be * p_inv, lo_half).astype(BF16)
        kt = _stack_heads(k * p_inv, lo_half).astype(BF16)
        bh = _stack_heads(be * p_rest, lo_half).astype(BF16)
        kh = _stack_heads(k * p_rest, lo_half).astype(BF16)
        vs = _stack_heads(v, lo_half).astype(BF16)

        cc = _dot_nt(jnp.concatenate([at, rt], axis=0).astype(BF16), jnp.concatenate([bt, kt], axis=0))
        yield
        l_ab = jnp.where(strict, cc[:2 * c, :2 * c], 0.0)
        l_ak = jnp.where(strict, cc[:2 * c, 2 * c:], 0.0)
        m_rb = jnp.where(incl, cc[2 * c:, :2 * c], 0.0)
        m_rk = jnp.where(incl, cc[2 * c:, 2 * c:], 0.0)

        lv = _dot(jnp.concatenate([l_ak, m_rk], axis=0).astype(BF16), vs)
        x = jnp.where(lo_lanes, fold(at), fold(lv[:2 * c, :]))
        yield
        lp = l_ab.astype(BF16)
        n = 1
        while n < c:
            x = x + _dot(lp, x.astype(BF16))
            yield
            n *= 2
            if n < c:
                lp = _dot(lp, lp).astype(BF16)
        x16 = x.astype(BF16)

        mw1, mw2 = unfold(_dot(m_rb.astype(BF16), x16))
        yield
        qp_ref[j] = (rt + mw1).astype(BF16)
        y1_ref[j] = mw2 + lv[2 * c:, :]
        bw1, bw2 = unfold(_dot_tn(bh, x16))
        p_end = jnp.exp(-cs_end)
        decay_end = jnp.where(diag, jnp.broadcast_to(p_end, (2 * c, LANES)), 0.0)
        mc_ref[j] = (decay_end + bw1).astype(BF16)
        nc_ref[j] = bw2 + _dot_tn(kh, vs)

    def chunk_group(i, carry):
        _run_interleaved([chunk(i * unroll + u) for u in range(unroll)])
        return carry

    lax.fori_loop(0, ts // (c * unroll), chunk_group, 0)


def rwkv_chunks(z, col0, w0, w_up, a0, a_up, g_up, k_k, k_a, r_k, c_heads, ts):
    b, s, _ = z.shape
    cw = c_heads * HEAD_DIM
    n_pairs = c_heads // 2
    c = RWKV_CHUNK
    n_chunks = s // c
    cpb = ts // c
    assert 2 * c == LANES and W_LORA + A_LORA == LANES and G_LORA == 2 * LANES
    col_lo = col0 + 3 * n_pairs
    zero = jnp.zeros((A_LORA, cw), F32)
    wup_ext = jnp.concatenate([w_up, zero], axis=0).astype(BF16)
    aup_ext = jnp.concatenate([zero, a_up], axis=0).astype(BF16)

    def tok(col):
        return pl.BlockSpec((None, ts, LANES), lambda bi, hp, t, col=col: (bi, t, col(hp)))

    cols = [lambda hp: col0 + hp, lambda hp: col0 + n_pairs + hp, lambda hp: col0 + 2 * n_pairs + hp,
            lambda hp: col_lo, lambda hp: col_lo + 1, lambda hp: col_lo + 2]

    def per_pair(rows):
        return pl.BlockSpec((rows, LANES), lambda bi, hp, t: (0, hp))

    mat = lambda dt: jax.ShapeDtypeStruct((b, n_pairs, n_chunks, LANES, LANES), dt)
    mat_spec = pl.BlockSpec((None, None, cpb, LANES, LANES), lambda bi, hp, t: (bi, hp, t, 0, 0))
    tok_out = pl.BlockSpec((None, ts, LANES), lambda bi, hp, t: (bi, t, hp))
    return pl.pallas_call(
        functools.partial(_rwkv_chunk_kernel, ts=ts, unroll=min(16, cpb)),
        grid=(b, n_pairs, s // ts),
        in_specs=[tok(cf) for cf in cols] + [
            per_pair(1), per_pair(LANES), per_pair(1), per_pair(LANES), per_pair(G_LORA),
            per_pair(1), per_pair(1), per_pair(1)],
        out_specs=[mat_spec, mat_spec, mat_spec, mat_spec, tok_out, tok_out],
        out_shape=[mat(BF16), mat(BF16), mat(F32), mat(F32),
                   jax.ShapeDtypeStruct((b, s, cw), F32), jax.ShapeDtypeStruct((b, s, cw), F32)],
        scratch_shapes=[pltpu.VMEM((ts, LANES), F32)] * 2,
        compiler_params=_params("parallel", "parallel", "parallel"),
        name="rwkv_chunks",
    )(z, z, z, z, z, z, w0.reshape(1, cw), wup_ext, a0.reshape(1, cw), aup_ext, g_up.astype(BF16),
      k_k.reshape(1, cw), k_a.reshape(1, cw), r_k.reshape(1, cw))


def _rwkv_state_kernel(mc_ref, qp_ref, nc_ref, y1_ref, bonus_ref, gate_ref, lg_ref, lb_ref, o_ref, h_ref,
                       *, n_pairs, cpb):
    c = RWKV_CHUNK

    @pl.when(pl.program_id(1) == 0)
    def _():
        h_ref[...] = jnp.zeros(h_ref.shape, F32)

    lane = lax.broadcasted_iota(jnp.int32, (c, LANES), 1)
    lo_half = lane < HEAD_DIM
    inv_n = 1.0 / HEAD_DIM

    def pair(g, j, rows):
        cols = slice(g * LANES, (g + 1) * LANES)
        both = _dot(jnp.concatenate([qp_ref[g, j], mc_ref[g, j]], axis=0), h_ref[g].astype(BF16))
        ys = both[:2 * c, :] + y1_ref[g, j]
        h_ref[g] = both[2 * c:, :] + nc_ref[g, j]
        yield
        y = ys[:c, :] + ys[c:, :]
        yc = y - _head_sums(y, lo_half) * inv_n
        yn = yc * lax.rsqrt(_head_sums(yc * yc, lo_half) * inv_n + RWKV_GN_EPS)
        yn = yn * lg_ref[:, cols] + lb_ref[:, cols]
        o_ref[rows, cols] = ((yn + bonus_ref[rows, cols]) * gate_ref[rows, cols]).astype(o_ref.dtype)

    def chunk(j, carry):
        rows = pl.ds(pl.multiple_of(j * c, c), c)
        _run_interleaved([pair(g, j, rows) for g in range(n_pairs)])
        return carry

    lax.fori_loop(0, cpb, chunk, 0)


def rwkv_state_scan(mc, qp, nc, y1, bonus, gate, ln_gain, ln_bias, cpb):
    b, n_pairs, n_chunks = mc.shape[:3]
    s = n_chunks * RWKV_CHUNK
    cw = n_pairs * LANES
    ts = cpb * RWKV_CHUNK
    mat_spec = pl.BlockSpec((None, n_pairs, cpb, LANES, LANES), lambda bi, t: (bi, 0, t, 0, 0))
    tok_spec = pl.BlockSpec((None, ts, cw), lambda bi, t: (bi, t, 0))
    vec_spec = pl.BlockSpec((1, cw), lambda bi, t: (0, 0))
    return pl.pallas_call(
        functools.partial(_rwkv_state_kernel, n_pairs=n_pairs, cpb=cpb),
        grid=(b, n_chunks // cpb),
        in_specs=[mat_spec, mat_spec, mat_spec, mat_spec, tok_spec, tok_spec, vec_spec, vec_spec],
        out_specs=tok_spec,
        out_shape=jax.ShapeDtypeStruct((b, s, cw), BF16),
        scratch_shapes=[pltpu.VMEM((n_pairs, LANES, LANES), F32)],
        compiler_params=_params("parallel", "arbitrary"),
        name="rwkv_state_scan",
    )(mc, qp, nc, y1, bonus, gate, ln_gain.reshape(1, cw), ln_bias.reshape(1, cw))


def _mixer_out_kernel(oa_ref, ob_ref, oc_ref, ag_ref, wa_ref, wb_ref, wc_ref, x_ref, g_ref, o_ref):
    oa = _rms_rows(oa_ref[...], ag_ref[...]).astype(BF16)
    y = _dot(oa, wa_ref[...]) + _dot(ob_ref[...], wb_ref[...]) + _dot(oc_ref[...], wc_ref[...])
    o_ref[...] = x_ref[...] + _rms_rows(y, g_ref[...])


def mixer_out(oa, ob, oc, attn_gain, w_out, x, gain, tm):
    m, d = x.shape
    aw, bw, cw = oa.shape[1], ob.shape[1], oc.shape[1]
    w = w_out.astype(BF16)
    row = lambda width: pl.BlockSpec((tm, width), lambda i: (i, 0))
    full = lambda r, cdim: pl.BlockSpec((r, cdim), lambda i: (0, 0))
    return pl.pallas_call(
        _mixer_out_kernel,
        grid=(m // tm,),
        in_specs=[row(aw), row(bw), row(cw), full(1, aw), full(aw, d), full(bw, d), full(cw, d),
                  row(d), full(1, d)],
        out_specs=row(d),
        out_shape=jax.ShapeDtypeStruct((m, d), F32),
        compiler_params=_params("parallel"),
        name="mixer_out",
    )(oa, ob, oc, attn_gain.reshape(1, aw), w[:aw], w[aw:aw + bw], w[aw + bw:], x, gain.reshape(1, d))


def _cross_attn_kernel(x_ref, gi_ref, wq_ref, kv_ref, wo_ref, go_ref, gn_ref, o_ref, hn_ref, *, heads):
    x = x_ref[...]
    q = _dot(_rms_rows(x, gi_ref[...]).astype(BF16), wq_ref[...])
    inner = heads * MEM_HEAD_DIM
    scale = MEM_HEAD_DIM ** -0.5
    outs = [None] * heads

    def head(h):
        cols = slice(h * MEM_HEAD_DIM, (h + 1) * MEM_HEAD_DIM)
        s = _dot_nt(q[:, cols].astype(BF16), kv_ref[:, cols]) * scale
        yield
        m = jnp.max(s, axis=-1, keepdims=True)
        e = jnp.exp(s - m)
        p = e * (1.0 / jnp.sum(e, axis=-1, keepdims=True))
        outs[h] = _dot(p.astype(BF16), kv_ref[:, inner + h * MEM_HEAD_DIM:inner + (h + 1) * MEM_HEAD_DIM])

    _run_interleaved([head(h) for h in range(heads)])
    o = jnp.concatenate(outs, axis=1).astype(BF16)
    x_out = x + _rms_rows(_dot(o, wo_ref[...]), go_ref[...])
    o_ref[...] = x_out
    hn_ref[...] = _rms_rows(x_out, gn_ref[...]).astype(hn_ref.dtype)


def cross_attention(x, gain_in, wq, kv, wo, gain_out, gain_next, seq, tm):
    m, d = x.shape
    inner = wq.shape[1]
    heads = inner // MEM_HEAD_DIM
    mem_tokens = kv.shape[0] // (m // seq)
    blocks_per_seq = seq // tm
    full = lambda r, cdim: pl.BlockSpec((r, cdim), lambda i: (0, 0))
    return pl.pallas_call(
        functools.partial(_cross_attn_kernel, heads=heads),
        grid=(m // tm,),
        in_specs=[pl.BlockSpec((tm, d), lambda i: (i, 0)), full(1, d), full(d, inner),
                  pl.BlockSpec((mem_tokens, 2 * inner), lambda i: (i // blocks_per_seq, 0)),
                  full(inner, d), full(1, d), full(1, d)],
        out_specs=[pl.BlockSpec((tm, d), lambda i: (i, 0)), pl.BlockSpec((tm, d), lambda i: (i, 0))],
        out_shape=[jax.ShapeDtypeStruct((m, d), F32), jax.ShapeDtypeStruct((m, d), BF16)],
        compiler_params=_params("parallel"),
        name="cross_attention",
    )(x, gain_in.reshape(1, d), wq.astype(BF16), kv, wo.astype(BF16), gain_out.reshape(1, d),
      gain_next.reshape(1, d))


def _conv_ffn_kernel(h_ref, hh_ref, wg_ref, wv_ref, cwg_ref, cwv_ref, cbg_ref, cbv_ref, wd_ref, x_ref, go_ref,
                     o_ref, hs_ref, acc_ref, inv_ref, *, blocks_per_seq, nf, n_col):
    i = pl.program_id(0)
    f = pl.program_id(1)
    tc = o_ref.shape[1]

    @pl.when(f == 0)
    def _():
        hs_ref[0:HALO, :] = jnp.where(i % blocks_per_seq == 0, jnp.zeros_like(hh_ref[...]), hh_ref[...])
        hs_ref[HALO:, :] = h_ref[...]
        acc_ref[...] = jnp.zeros(acc_ref.shape, F32)

    @pl.when(f < nf)
    def _():
        def conv(w_ref, cw_ref, cb_ref):
            up = _dot(hs_ref[...], w_ref[...])
            return (cb_ref[...] + pltpu.roll(up, 2, axis=0) * cw_ref[0:1, :]
                    + pltpu.roll(up, 1, axis=0) * cw_ref[1:2, :] + up * cw_ref[2:3, :])

        gate = conv(wg_ref, cwg_ref, cbg_ref)
        val = conv(wv_ref, cwv_ref, cbv_ref)
        act = (_gelu_tanh(gate) * val)[HALO:, :].astype(BF16)
        for c in range(n_col):
            acc_ref[c] += _dot(act, wd_ref[:, c * tc:(c + 1) * tc])

    @pl.when(f == nf)
    def _():
        ss = jnp.zeros((acc_ref.shape[1], 1), F32)
        for c in range(n_col):
            y = acc_ref[c]
            ss = ss + jnp.sum(y * y, axis=-1, keepdims=True)
        inv_ref[...] = lax.rsqrt(ss * (1.0 / (n_col * tc)) + NORM_EPS)

    @pl.when(f >= nf)
    def _():
        o_ref[...] = x_ref[...] + acc_ref[f - nf] * inv_ref[...] * go_ref[...]


def conv_ffn(h, x, conv_w, conv_b, w_up, w_down, gain_out, seq, tm, tf, tc):
    m, d = x.shape
    d_ff = w_down.shape[0]
    nf = d_ff // tf
    n_col = d // tc
    assert d_ff % tf == 0 and d % tc == 0 and seq % tm == 0
    assert conv_w.shape[0] == CONV_WIDTH and CONV_WIDTH - 1 <= HALO
    wu = w_up.astype(BF16)
    cb = conv_b.reshape(1, 2 * d_ff)
    up_blk = lambda f: jnp.minimum(f, nf - 1)
    col_blk = lambda f: jnp.maximum(f - nf, 0)
    return pl.pallas_call(
        functools.partial(_conv_ffn_kernel, blocks_per_seq=seq // tm, nf=nf, n_col=n_col),
        grid=(m // tm, nf + n_col),
        in_specs=[pl.BlockSpec((tm, d), lambda i, f: (i, 0)),
                  pl.BlockSpec((HALO, d), lambda i, f: (jnp.maximum(i * (tm // HALO) - 1, 0), 0)),
                  pl.BlockSpec((d, tf), lambda i, f: (0, up_blk(f))),
                  pl.BlockSpec((d, tf), lambda i, f: (0, nf + up_blk(f))),
                  pl.BlockSpec((CONV_WIDTH, tf), lambda i, f: (0, up_blk(f))),
                  pl.BlockSpec((CONV_WIDTH, tf), lambda i, f: (0, nf + up_blk(f))),
                  pl.BlockSpec((1, tf), lambda i, f: (0, up_blk(f))),
                  pl.BlockSpec((1, tf), lambda i, f: (0, nf + up_blk(f))),
                  pl.BlockSpec((tf, d), lambda i, f: (up_blk(f), 0)),
                  pl.BlockSpec((tm, tc), lambda i, f: (i, col_blk(f))),
                  pl.BlockSpec((1, tc), lambda i, f: (0, col_blk(f)))],
        out_specs=pl.BlockSpec((tm, tc), lambda i, f: (i, col_blk(f))),
        out_shape=jax.ShapeDtypeStruct((m, d), F32),
        scratch_shapes=[pltpu.VMEM((tm + HALO, d), BF16), pltpu.VMEM((n_col, tm, tc), F32),
                        pltpu.VMEM((tm, 1), F32)],
        compiler_params=_params("parallel", "arbitrary"),
        name="conv_ffn",
    )(h, h, wu, wu, conv_w, conv_w, cb, cb, w_down.astype(BF16), x, gain_out.reshape(1, d))


def _tile_sizes(seq):
    tm = 512 if seq % 512 == 0 else seq
    return dict(tm_tall=min(1024, seq), tn_proj=1024, tm=tm, tf=512, tc=512, ts=min(1024, seq), cpb=min(16, seq // RWKV_CHUNK))


def kernel(x, mem, rel_bias_table, sandwich_gains, mem_src_gain, w_in, w_out, attn_out_gain, sgu_norm_gain, sgu_w, sgu_b, sgu_out_gain, rwkv_mu, rwkv_w0, rwkv_w_up, rwkv_a0, rwkv_a_up, rwkv_g_up, rwkv_k_k, rwkv_k_a, rwkv_r_k, rwkv_ln_gain, rwkv_ln_bias, mem_wq, mem_wkv, mem_wo, ffn_w_up, ffn_conv_w, ffn_conv_b, ffn_w_down):
    b, s, d = x.shape
    depth = w_in.shape[0]
    a_heads = rel_bias_table.shape[1]
    aw = a_heads * HEAD_DIM
    bw = sgu_norm_gain.shape[1]
    cw = rwkv_w0.shape[1]
    c_heads = cw // HEAD_DIM
    t = _tile_sizes(s)
    m = b * s
    xf = x.reshape(m, d)
    memf = mem.reshape(b * mem.shape[1], d)
    for l in range(depth):
        g = sandwich_gains[l]
        wi = w_in[l]
        n_in = wi.shape[1]
        n_pad = -n_in % t["tn_proj"]
        wi = jnp.concatenate([wi[:, 3 * aw:3 * aw + 2 * bw], wi[:, :3 * aw], wi[:, 3 * aw + 2 * bw:],
                              jnp.zeros((d, n_pad), wi.dtype)], axis=1).astype(BF16)
        col_a = 2 * bw // LANES
        col_c = col_a + 3 * aw // LANES
        mu = jnp.concatenate([jnp.zeros((2 * bw + 3 * aw,), F32), rwkv_mu[l], jnp.zeros((n_pad,), F32)])
        z = in_projection(xf, g[0], wi, mu, s, t["tm_tall"], t["tn_proj"])
        z3 = z.reshape(b, s, n_in + n_pad)
        oa = dilated_attention(z3, col_a, rel_bias_table, a_heads)
        ob = spatial_gating(z, 0, sgu_norm_gain[l], sgu_w[l], sgu_b[l], sgu_out_gain[l], t["tm"])
        mc, qp, nc, y1, bonus, gate = rwkv_chunks(
            z3, col_c, rwkv_w0[l], rwkv_w_up[l], rwkv_a0[l], rwkv_a_up[l], rwkv_g_up[l],
            rwkv_k_k[l], rwkv_k_a[l], rwkv_r_k[l], c_heads, t["ts"])
        oc = rwkv_state_scan(mc, qp, nc, y1, bonus, gate, rwkv_ln_gain[l], rwkv_ln_bias[l], t["cpb"])
        xf = mixer_out(oa.reshape(m, aw), ob, oc.reshape(m, cw), attn_out_gain[l], w_out[l], xf, g[1], t["tm"])
        kv = rms_matmul(memf, mem_src_gain[l], mem_wkv[l].astype(BF16), memf.shape[0], mem_wkv.shape[2] // 2,
                        out_dtype=BF16)
        xf, hf = cross_attention(xf, g[2], mem_wq[l], kv, mem_wo[l], g[3], g[4], s, t["tm"])
        xf = conv_ffn(hf, xf, ffn_conv_w[l], ffn_conv_b[l], ffn_w_up[l], ffn_w_down[l], g[5], s,
                      t["tm_tall"], t["tf"], t["tc"])
    return xf.reshape(b, s, d)
```
